```python
import math
import jax, jax.numpy as jnp
from jax import lax
import numpy as np


D_MODEL = 1024
BATCH = 4
SEQ = 8192
DEPTH = 2

A_HEADS = 8
A_HEAD_DIM = 64
A_V_DIM = 2 * A_HEAD_DIM
A_WIDTH = A_HEADS * A_V_DIM
B_HEADS = 8
B_QK_DIM = 64
B_V_DIM = 128
B_WIDTH = B_HEADS * B_V_DIM
CONV_WIDTH = 4
CHUNK = 64
Q_BLOCK = 128
ROPE_THETA = 10000.0
EPS = 1e-6
NEG_INIT = -1e30

SPLITS = (
    A_HEADS * 2 * A_HEAD_DIM,
    A_HEADS * 2 * A_HEAD_DIM,
    A_WIDTH,
    A_WIDTH,
    2 * B_HEADS * B_QK_DIM,
    B_WIDTH,
    B_HEADS,
    B_HEADS,
    B_WIDTH,
    B_WIDTH,
    D_MODEL,
    D_MODEL,
)
D_IN = sum(SPLITS)

kernel_name = 'hybrid_diffattn_mlstm_gated_block'


def rmsnorm(x, g):
    xf = x.astype(jnp.float32)
    y = xf * lax.rsqrt(jnp.mean(xf * xf, axis=-1, keepdims=True) + EPS)
    return y.astype(x.dtype) * g


def rope(x, cos, sin):
    half = x.shape[-1] // 2
    x1, x2 = x[..., :half], x[..., half:]
    cos = cos.astype(x.dtype)
    sin = sin.astype(x.dtype)
    return jnp.concatenate([x1 * cos - x2 * sin, x2 * cos + x1 * sin], axis=-1)


def causal_conv(x, w, b):
    c = x.shape[-1]
    y = lax.conv_general_dilated(
        x, w[:, None, :].astype(x.dtype), window_strides=(1,), padding=[(CONV_WIDTH - 1, 0)],
        dimension_numbers=('NWC', 'WIO', 'NWC'), feature_group_count=c)
    return y + b


def diff_attention(q, k, v, lam):
    bsz, nh, s, _, dh = q.shape
    dv = v.shape[-1]
    nqb = s // Q_BLOCK
    scale = dh ** -0.5
    kpos = jnp.arange(s)

    def block(j):
        start = j * Q_BLOCK
        qb = lax.dynamic_slice_in_dim(q, start, Q_BLOCK, axis=2)
        sc = jnp.einsum('bhqrd,bhkrd->bhrqk', qb, k).astype(jnp.float32) * scale
        qpos = start + jnp.arange(Q_BLOCK)
        causal = kpos[None, :] <= qpos[:, None]
        p = jax.nn.softmax(jnp.where(causal, sc, -jnp.inf), axis=-1)
        a = p[:, :, 0] - lam * p[:, :, 1]
        return jnp.einsum('bhqk,bhkd->bhqd', a.astype(v.dtype), v)

    out = lax.map(block, jnp.arange(nqb))
    return jnp.transpose(out, (1, 2, 0, 3, 4)).reshape(bsz, nh, s, dv)


def mlstm_chunkwise(q, k, v, i_pre, f_pre):
    bsz, nh, s, dk = q.shape
    dv = v.shape[-1]
    nc = s // CHUNK
    f32 = jnp.float32
    q = q.astype(f32).reshape(bsz, nh, nc, CHUNK, dk) * (dk ** -0.5)
    k = k.astype(f32).reshape(bsz, nh, nc, CHUNK, dk)
    v = v.astype(f32).reshape(bsz, nh, nc, CHUNK, dv)
    ig = i_pre.astype(f32).reshape(bsz, nh, nc, CHUNK)
    b = jnp.cumsum(jax.nn.log_sigmoid(f_pre.astype(f32)).reshape(bsz, nh, nc, CHUNK), axis=-1)
    b_last = b[..., -1]
    w_log = b_last[..., None] - b + ig
    m_loc = jnp.max(w_log, axis=-1)
    wk = jnp.exp(w_log - m_loc[..., None])[..., None] * k
    c_loc = jnp.einsum('bhcld,bhcle->bhcde', wk, v)
    n_loc = jnp.sum(wk, axis=3)

    def step(carry, xs):
        c_st, n_st, m_st = carry
        bl, ml, cl, nl = xs
        m_new = jnp.maximum(bl + m_st, ml)
        a = jnp.exp(bl + m_st - m_new)
        e = jnp.exp(ml - m_new)
        c_new = a[..., None, None] * c_st + e[..., None, None] * cl
        n_new = a[..., None] * n_st + e[..., None] * nl
        return (c_new, n_new, m_new), (c_st, n_st, m_st)

    init = (jnp.zeros((bsz, nh, dk, dv), f32), jnp.zeros((bsz, nh, dk), f32), jnp.full((bsz, nh), NEG_INIT, f32))
    xs = (jnp.moveaxis(b_last, 2, 0), jnp.moveaxis(m_loc, 2, 0), jnp.moveaxis(c_loc, 2, 0), jnp.moveaxis(n_loc, 2, 0))
    _, (c_prev, n_prev, m_prev) = lax.scan(step, init, xs)
    c_prev = jnp.moveaxis(c_prev, 0, 2)
    n_prev = jnp.moveaxis(n_prev, 0, 2)
    m_prev = jnp.moveaxis(m_prev, 0, 2)
    causal = jnp.tril(jnp.ones((CHUNK, CHUNK), dtype=bool))
    d_log = jnp.where(causal, b[..., :, None] - b[..., None, :] + ig[..., None, :], -jnp.inf)
    inter_log = b + m_prev[..., None]
    m_t = jnp.maximum(inter_log, jnp.max(d_log, axis=-1))
    scores = jnp.einsum('bhctd,bhcsd->bhcts', q, k) * jnp.exp(d_log - m_t[..., None])
    inter_w = jnp.exp(inter_log - m_t)
    num = jnp.einsum('bhcts,bhcse->bhcte', scores, v) + inter_w[..., None] * jnp.einsum('bhctd,bhcde->bhcte', q, c_prev)
    den = jnp.sum(scores, axis=-1) + inter_w * jnp.einsum('bhctd,bhcd->bhct', q, n_prev)
    h = num / jnp.maximum(jnp.abs(den), jnp.exp(-m_t))[..., None]
    return h.reshape(bsz, nh, s, dv)


def hybrid_layer(x, cos, sin, layer_idx, norm_g, w_in, q_norm_g, k_norm_g, lambda_qk, attn_norm_g, w_out_a,
                 conv_w, conv_b, igate_b, fgate_b, mlstm_norm_g, w_out_b, w_o):
    bsz, s, _ = x.shape
    h = rmsnorm(x, norm_g)
    proj = h @ w_in
    idx = [int(i) for i in np.cumsum(SPLITS)[:-1]]
    aq, ak, av, az, bqk, bv, bi, bf, bo, bz, ga, gb = jnp.split(proj, idx, axis=-1)

    def two_map_heads(t):
        return t.reshape(bsz, s, A_HEADS, 2, A_HEAD_DIM).transpose(0, 2, 1, 3, 4)
    q = rope(rmsnorm(two_map_heads(aq), q_norm_g), cos, sin)
    k = rope(rmsnorm(two_map_heads(ak), k_norm_g), cos, sin)
    v = av.reshape(bsz, s, A_HEADS, A_V_DIM).transpose(0, 2, 1, 3)
    lam_init = 0.8 - 0.6 * math.exp(-0.3 * layer_idx)
    lq1, lk1, lq2, lk2 = lambda_qk[0], lambda_qk[1], lambda_qk[2], lambda_qk[3]
    lam = jnp.exp(jnp.sum(lq1 * lk1)) - jnp.exp(jnp.sum(lq2 * lk2)) + lam_init
    oa = diff_attention(q, k, v, lam)
    oa = rmsnorm(oa, attn_norm_g) * (1.0 - lam_init)
    oa = oa.transpose(0, 2, 1, 3).reshape(bsz, s, A_WIDTH) * jax.nn.silu(az)
    ya = oa @ w_out_a

    qk = jax.nn.silu(causal_conv(bqk, conv_w, conv_b))
    mq, mk = jnp.split(qk, 2, axis=-1)
    mq = mq.reshape(bsz, s, B_HEADS, B_QK_DIM).transpose(0, 2, 1, 3)
    mk = mk.reshape(bsz, s, B_HEADS, B_QK_DIM).transpose(0, 2, 1, 3)
    mv = bv.reshape(bsz, s, B_HEADS, B_V_DIM).transpose(0, 2, 1, 3)
    i_pre = (bi + igate_b).transpose(0, 2, 1)
    f_pre = (bf + fgate_b).transpose(0, 2, 1)
    hb = mlstm_chunkwise(mq, mk, mv, i_pre, f_pre).astype(x.dtype)
    hb = rmsnorm(hb, mlstm_norm_g).transpose(0, 2, 1, 3).reshape(bsz, s, B_WIDTH)
    hb = jax.nn.sigmoid(bo) * hb * jax.nn.silu(bz)
    yb = hb @ w_out_b

    u = jax.nn.sigmoid(ga) * ya + jax.nn.sigmoid(gb) * yb
    return x + u @ w_o


def setup_inputs(seed: int = 0) -> dict:
    key = jax.random.key(seed)
    ks = jax.random.split(key, 16)
    f32 = jnp.float32
    x = jax.random.normal(ks[0], (BATCH, SEQ, D_MODEL), f32)
    offsets = jax.random.randint(ks[1], (BATCH, 1), 0, 4096, dtype=jnp.int32)
    positions = (offsets + jnp.arange(SEQ, dtype=jnp.int32)[None, :]).astype(jnp.int32)
    norm_g = 1.0 + 0.02 * jax.random.normal(ks[2], (DEPTH, D_MODEL), f32)
    w_in = jax.random.normal(ks[3], (DEPTH, D_MODEL, D_IN), f32) * D_MODEL ** -0.5
    q_norm_g = 1.0 + 0.02 * jax.random.normal(ks[4], (DEPTH, A_HEAD_DIM), f32)
    k_norm_g = 1.0 + 0.02 * jax.random.normal(ks[5], (DEPTH, A_HEAD_DIM), f32)
    lambda_qk = 0.1 * jax.random.normal(ks[6], (DEPTH, 4, A_HEAD_DIM), f32)
    attn_norm_g = 1.0 + 0.02 * jax.random.normal(ks[7], (DEPTH, A_V_DIM), f32)
    w_out_a = jax.random.normal(ks[8], (DEPTH, A_WIDTH, D_MODEL), f32) * A_WIDTH ** -0.5
    conv_w = jax.random.normal(ks[9], (DEPTH, CONV_WIDTH, 2 * B_HEADS * B_QK_DIM), f32) * CONV_WIDTH ** -0.5
    conv_b = 0.02 * jax.random.normal(ks[10], (DEPTH, 2 * B_HEADS * B_QK_DIM), f32)
    igate_b = 0.1 * jax.random.normal(ks[11], (DEPTH, B_HEADS), f32)
    fgate_b = jnp.linspace(3.0, 6.0, B_HEADS, dtype=f32)[None, :] + 0.1 * jax.random.normal(ks[12], (DEPTH, B_HEADS), f32)
    mlstm_norm_g = 1.0 + 0.02 * jax.random.normal(ks[13], (DEPTH, B_V_DIM), f32)
    w_out_b = jax.random.normal(ks[14], (DEPTH, B_WIDTH, D_MODEL), f32) * B_WIDTH ** -0.5
    w_o = jax.random.normal(ks[15], (DEPTH, D_MODEL, D_MODEL), f32) * D_MODEL ** -0.5
    return {'x': x, 'positions': positions, 'norm_g': norm_g, 'w_in': w_in, 'q_norm_g': q_norm_g,
            'k_norm_g': k_norm_g, 'lambda_qk': lambda_qk, 'attn_norm_g': attn_norm_g, 'w_out_a': w_out_a,
            'conv_w': conv_w, 'conv_b': conv_b, 'igate_b': igate_b, 'fgate_b': fgate_b,
            'mlstm_norm_g': mlstm_norm_g, 'w_out_b': w_out_b, 'w_o': w_o}


def reference(x, positions, norm_g, w_in, q_norm_g, k_norm_g, lambda_qk, attn_norm_g, w_out_a,
              conv_w, conv_b, igate_b, fgate_b, mlstm_norm_g, w_out_b, w_o):
    inv_freq = ROPE_THETA ** (-jnp.arange(0, A_HEAD_DIM, 2, dtype=jnp.float32) / A_HEAD_DIM)
    ang = positions.astype(jnp.float32)[..., None] * inv_freq
    cos = jnp.cos(ang)[:, None, :, None, :]
    sin = jnp.sin(ang)[:, None, :, None, :]
    for l in range(DEPTH):
        x = hybrid_layer(x, cos, sin, l, norm_g[l], w_in[l], q_norm_g[l], k_norm_g[l], lambda_qk[l],
                         attn_norm_g[l], w_out_a[l], conv_w[l], conv_b[l], igate_b[l], fgate_b[l],
                         mlstm_norm_g[l], w_out_b[l], w_o[l])
    return x
```

```python
import functools
import math

import jax
import jax.numpy as jnp
from jax import lax
from jax.experimental import pallas as pl
from jax.experimental.pallas import tpu as pltpu

F32 = jnp.float32
BF16 = jnp.bfloat16

D_MODEL = 1024
HEADS = 8
QK_DIM = 64
V_DIM = 128
CONV_WIDTH = 4
ROPE_THETA = 10000.0
EPS = 1e-6
NEG_INIT = -1e30
LANES = 128
SUBLANES = 8
VMEM_LIMIT = 48 * 1024 * 1024

SLAB_AQ, SLAB_AK, SLAB_AV, SLAB_AZ = 0, 8, 16, 24
SLAB_BQK, SLAB_BV, SLAB_BO, SLAB_BZ, SLAB_GA, SLAB_GB = 32, 40, 48, 56, 64, 72
N_SLABS = 80
N_MAIN = N_SLABS * LANES

_REF_SPLITS = (1024, 1024, 1024, 1024, 1024, 1024, 8, 8, 1024, 1024, 1024, 1024)


def _split_hi_lo(x):
    hi = x.astype(BF16)
    lo = (x - hi.astype(F32)).astype(BF16)
    return hi, lo


def _sigmoid(x):
    return 1.0 / (1.0 + jnp.exp(-x))


def _inproj_kernel(x_ref, g_ref, w_ref, wg_ref, p_ref, gate_ref, h_ref, *, n_sub):
    @pl.when(pl.program_id(1) == 0)
    def _():
        x = x_ref[...]
        ms = jnp.mean(x * x, axis=-1, keepdims=True)
        h = ((x * lax.rsqrt(ms + EPS)) * g_ref[...]).astype(BF16)
        h_ref[...] = h
        gate_ref[...] = jnp.dot(h, wg_ref[...], preferred_element_type=F32)

    acc = jnp.dot(h_ref[...], w_ref[...], preferred_element_type=F32)
    for c in range(n_sub):
        p_ref[c] = acc[:, c * LANES:(c + 1) * LANES].astype(BF16)


def _inproj(x2, norm_g, w_main, w_gate, *, tm, tn):
    m = x2.shape[0]
    n_sub = tn // LANES
    return pl.pallas_call(
        functools.partial(_inproj_kernel, n_sub=n_sub),
        grid=(m // tm, N_MAIN // tn),
        in_specs=[
            pl.BlockSpec((tm, D_MODEL), lambda i, j: (i, 0)),
            pl.BlockSpec((1, D_MODEL), lambda i, j: (0, 0)),
            pl.BlockSpec((D_MODEL, tn), lambda i, j: (0, j)),
            pl.BlockSpec((D_MODEL, LANES), lambda i, j: (0, 0)),
        ],
        out_specs=[
            pl.BlockSpec((n_sub, tm, LANES), lambda i, j: (j, i, 0)),
            pl.BlockSpec((tm, LANES), lambda i, j: (i, 0)),
        ],
        out_shape=[
            jax.ShapeDtypeStruct((N_SLABS, m, LANES), BF16),
            jax.ShapeDtypeStruct((m, LANES), F32),
        ],
        scratch_shapes=[pltpu.VMEM((tm, D_MODEL), BF16)],
        compiler_params=pltpu.CompilerParams(
            dimension_semantics=("arbitrary", "arbitrary"), vmem_limit_bytes=VMEM_LIMIT),
        name="inproj",
    )(x2, norm_g, w_main, w_gate)


def _qkprep_kernel(p_ref, cos_ref, sin_ref, g_ref, ones_ref, o_ref):
    x = p_ref[...].astype(F32)
    hi, lo = _split_hi_lo(x * x)
    ones = ones_ref[...]
    ss = (jnp.dot(hi, ones, preferred_element_type=F32)
          + jnp.dot(lo, ones, preferred_element_type=F32))
    y = (x * lax.rsqrt(ss * (1.0 / QK_DIM) + EPS)) * g_ref[...]
    lane = lax.broadcasted_iota(jnp.int32, y.shape, 1)
    half = QK_DIM // 2
    partner = jnp.where((lane & half) == 0,
                        pltpu.roll(y, LANES - half, axis=1),
                        pltpu.roll(y, half, axis=1))
    o_ref[...] = (y * cos_ref[...] + partner * sin_ref[...]).astype(BF16)


def _qkprep(p, cos4, sin4, gains, ones_blk, *, tm):
    m = p.shape[1]
    n_qk = 2 * HEADS
    return pl.pallas_call(
        _qkprep_kernel,
        grid=(m // tm, n_qk),
        in_specs=[
            pl.BlockSpec((None, tm, LANES), lambda i, c: (c, i, 0)),
            pl.BlockSpec((tm, LANES), lambda i, c: (i, 0)),
            pl.BlockSpec((tm, LANES), lambda i, c: (i, 0)),
            pl.BlockSpec((None, 1, LANES), lambda i, c: (c, 0, 0)),
            pl.BlockSpec((LANES, LANES), lambda i, c: (0, 0)),
        ],
        out_specs=pl.BlockSpec((None, tm, LANES), lambda i, c: (c, i, 0)),
        out_shape=jax.ShapeDtypeStruct((n_qk, m, LANES), BF16),
        compiler_params=pltpu.CompilerParams(
            dimension_semantics=("arbitrary", "arbitrary"), vmem_limit_bytes=VMEM_LIMIT),
        name="qkprep",
    )(p, cos4, sin4, gains, ones_blk)


def _attn_kernel(qi_tab, ki_tab, q_ref, k_ref, v_ref, az_ref, lam_ref, g_ref, o_ref,
                 qs_ref, m_ref, l_ref, acc_ref, *, blk, lam_init):
    t = pl.program_id(1)
    qi = qi_tab[t]
    ki = ki_tab[t]

    @pl.when(ki == 0)
    def _():
        q = q_ref[...]
        lane = lax.broadcasted_iota(jnp.int32, q.shape, 1)
        zero = jnp.zeros_like(q)
        qs_ref[0:blk, :] = jnp.where(lane < QK_DIM, q, zero)
        qs_ref[blk:2 * blk, :] = jnp.where(lane >= QK_DIM, q, zero)
        m_ref[...] = jnp.full(m_ref.shape, -jnp.inf, F32)
        l_ref[...] = jnp.zeros(l_ref.shape, F32)
        acc_ref[...] = jnp.zeros(acc_ref.shape, F32)

    def scores():
        return lax.dot_general(qs_ref[...], k_ref[...], (((1,), (1,)), ((), ())),
                               preferred_element_type=F32)

    def update(s):
        m_prev = m_ref[...]
        m_new = jnp.maximum(m_prev, jnp.max(s, axis=-1, keepdims=True))
        alpha = jnp.exp(m_prev - m_new)
        p = jnp.exp(s - m_new)
        l_ref[...] = alpha * l_ref[...] + jnp.sum(p, axis=-1, keepdims=True)
        acc_ref[...] = alpha * acc_ref[...] + jnp.dot(p.astype(BF16), v_ref[...],
                                                      preferred_element_type=F32)
        m_ref[...] = m_new

    @pl.when(ki < qi)
    def _():
        update(scores())

    @pl.when(ki == qi)
    def _():
        s = scores()
        row = lax.broadcasted_iota(jnp.int32, s.shape, 0) & (blk - 1)
        col = lax.broadcasted_iota(jnp.int32, s.shape, 1)
        update(jnp.where(col <= row, s, -jnp.inf))

        lq = lam_ref[...]
        lam = (jnp.exp(jnp.sum(lq[0:1] * lq[1:2], axis=-1, keepdims=True))
               - jnp.exp(jnp.sum(lq[2:3] * lq[3:4], axis=-1, keepdims=True)) + lam_init)
        o = acc_ref[...] / l_ref[...]
        o = o[0:blk] - lam * o[blk:2 * blk]
        ms = jnp.mean(o * o, axis=-1, keepdims=True)
        y = (o * lax.rsqrt(ms + EPS)) * g_ref[...] * (1.0 - lam_init)
        az = az_ref[...].astype(F32)
        o_ref[...] = (y * (az * _sigmoid(az))).astype(BF16)


def _attention(qk, p, lambda_qk, attn_norm_g, *, bsz, seq, blk, lam_init):
    m = bsz * seq
    nb = seq // blk
    pairs = [(qi, ki) for qi in range(nb) for ki in range(qi + 1)]
    qi_tab = jnp.asarray([a for a, _ in pairs], jnp.int32)
    ki_tab = jnp.asarray([b for _, b in pairs], jnp.int32)

    def q_map(bh, t, qt, kt):
        return (bh % HEADS, (bh // HEADS) * nb + qt[t], 0)

    def k_map(bh, t, qt, kt):
        return (HEADS + bh % HEADS, (bh // HEADS) * nb + kt[t], 0)

    def v_map(bh, t, qt, kt):
        return (SLAB_AV + bh % HEADS, (bh // HEADS) * nb + kt[t], 0)

    def az_map(bh, t, qt, kt):
        return (SLAB_AZ + bh % HEADS, (bh // HEADS) * nb + qt[t], 0)

    grid_spec = pltpu.PrefetchScalarGridSpec(
        num_scalar_prefetch=2,
        grid=(bsz * HEADS, len(pairs)),
        in_specs=[
            pl.BlockSpec((None, blk, LANES), q_map),
            pl.BlockSpec((None, blk, LANES), k_map),
            pl.BlockSpec((None, blk, LANES), v_map),
            pl.BlockSpec((None, blk, LANES), az_map),
            pl.BlockSpec((4, QK_DIM), lambda bh, t, qt, kt: (0, 0)),
            pl.BlockSpec((1, V_DIM), lambda bh, t, qt, kt: (0, 0)),
        ],
        out_specs=pl.BlockSpec((None, blk, LANES), q_map),
        scratch_shapes=[
            pltpu.VMEM((2 * blk, LANES), BF16),
            pltpu.VMEM((2 * blk, 1), F32),
            pltpu.VMEM((2 * blk, 1), F32),
            pltpu.VMEM((2 * blk, V_DIM), F32),
        ],
    )
    return pl.pallas_call(
        functools.partial(_attn_kernel, blk=blk, lam_init=lam_init),
        grid_spec=grid_spec,
        out_shape=jax.ShapeDtypeStruct((HEADS, m, LANES), BF16),
        compiler_params=pltpu.CompilerParams(
            dimension_semantics=("arbitrary", "arbitrary"), vmem_limit_bytes=VMEM_LIMIT),
        name="diffattn",
    )(qi_tab, ki_tab, qk, qk, p, p, lambda_qk, attn_norm_g)


def _log_sigmoid(x):
    return jnp.minimum(x, 0.0) - jnp.log(1.0 + jnp.exp(-jnp.abs(x)))


def _mlstm_kernel(qk_ref, v_ref, bo_ref, bz_ref, gate_ref, gbias_ref, cw_ref, cb_ref, ng_ref,
                  o_ref, tail_ref, c_ref, n_ref, m_ref, *, chunk):
    L = chunk
    n_pair = HEADS // 2

    @pl.when(pl.program_id(1) == 0)
    def _():
        tail_ref[...] = jnp.zeros(tail_ref.shape, F32)
        c_ref[...] = jnp.zeros(c_ref.shape, F32)
        n_ref[...] = jnp.zeros(n_ref.shape, F32)
        m_ref[...] = jnp.full(m_ref.shape, NEG_INIT, F32)

    row8 = lax.broadcasted_iota(jnp.int32, (SUBLANES, LANES), 0)
    conv = []
    for sl in range(2 * n_pair):
        x = qk_ref[sl].astype(F32)
        tail = tail_ref[sl]
        y = x * cw_ref[(CONV_WIDTH - 1) * 8 + sl: (CONV_WIDTH - 1) * 8 + sl + 1, :] + cb_ref[sl:sl + 1, :]
        for d in range(1, CONV_WIDTH):
            xr = pltpu.roll(x, d, axis=0)
            tr = pltpu.roll(tail, d, axis=0)
            first = jnp.where(row8 < d, tr, xr[0:SUBLANES])
            xs = jnp.concatenate([first, xr[SUBLANES:]], axis=0)
            j = CONV_WIDTH - 1 - d
            y = y + xs * cw_ref[j * 8 + sl: j * 8 + sl + 1, :]
        tail_ref[sl] = x[L - SUBLANES:L]
        conv.append(y * _sigmoid(y))
    lane = lax.broadcasted_iota(jnp.int32, (L, LANES), 1)
    lo_half = lane < QK_DIM

    g = gate_ref[...] + gbias_ref[...]
    ls = _log_sigmoid(g)
    r_i = lax.broadcasted_iota(jnp.int32, (L, L), 0)
    c_i = lax.broadcasted_iota(jnp.int32, (L, L), 1)
    causal = c_i <= r_i
    tri = jnp.where(causal, 1.0, 0.0).astype(BF16)
    ls_hi, ls_lo = _split_hi_lo(ls)
    bcum = (jnp.dot(tri, ls_hi, preferred_element_type=F32)
            + jnp.dot(tri, ls_lo, preferred_element_type=F32))
    g_t = jnp.transpose(g)
    b_t = jnp.transpose(bcum)

    for pr in range(n_pair):
        q_slab = conv[pr] * (QK_DIM ** -0.5)
        k_slab = conv[n_pair + pr]
        q_bf = q_slab.astype(BF16)
        zero = jnp.zeros_like(q_bf)
        q_stack = jnp.concatenate([jnp.where(lo_half, q_bf, zero), jnp.where(lo_half, zero, q_bf)], axis=0)
        k_bf = k_slab.astype(BF16)
        s_pair = lax.dot_general(q_stack, k_bf, (((1,), (1,)), ((), ())), preferred_element_type=F32)
        c_pair = c_ref[pr]
        qc_pair = jnp.dot(q_stack, c_pair.astype(BF16), preferred_element_type=F32)
        n_pair_row = n_ref[pr]
        qn = q_slab * n_pair_row

        w_cols = []
        a_vals = []
        for half in range(2):
            h = 2 * pr + half
            bcol = bcum[:, HEADS + h: HEADS + h + 1]
            icol = g[:, h: h + 1]
            brow = b_t[HEADS + h: HEADS + h + 1, :]
            irow = g_t[h: h + 1, :]
            blast = brow[:, L - 1: L]
            m_prev = m_ref[h]

            dmat = jnp.where(causal, bcol + (irow - brow), -jnp.inf)
            inter_log = bcol + m_prev
            m_t = jnp.maximum(inter_log, jnp.max(dmat, axis=-1, keepdims=True))
            sc = s_pair[half * L:(half + 1) * L] * jnp.exp(dmat - m_t)
            inter_w = jnp.exp(inter_log - m_t)
            v_h = v_ref[h]
            num = (jnp.dot(sc.astype(BF16), v_h, preferred_element_type=F32)
                   + inter_w * qc_pair[half * L:(half + 1) * L])
            qn_h = jnp.sum(jnp.where(lo_half if half == 0 else ~lo_half, qn, 0.0), axis=-1, keepdims=True)
            den = jnp.sum(sc, axis=-1, keepdims=True) + inter_w * qn_h
            hh = num / jnp.maximum(jnp.abs(den), jnp.exp(-m_t))

            ms = jnp.mean(hh * hh, axis=-1, keepdims=True)
            y = (hh * lax.rsqrt(ms + EPS)) * ng_ref[...]
            bo = bo_ref[h].astype(F32)
            bz = bz_ref[h].astype(F32)
            o_ref[h] = (_sigmoid(bo) * y * (bz * _sigmoid(bz))).astype(BF16)

            m_loc = jnp.max(blast - brow + irow, axis=-1, keepdims=True)
            m_new = jnp.maximum(blast + m_prev, m_loc)
            a_vals.append(jnp.exp(blast + m_prev - m_new))
            w_cols.append(jnp.exp((blast - m_new) - bcol + icol))
            m_ref[h] = m_new

        wk = k_slab * jnp.where(lo_half, w_cols[0], w_cols[1])
        wk_bf = wk.astype(BF16)
        zero = jnp.zeros_like(wk_bf)
        upd = (lax.dot_general(jnp.where(lo_half, wk_bf, zero), v_ref[2 * pr], (((0,), (0,)), ((), ())),
                               preferred_element_type=F32)
               + lax.dot_general(jnp.where(lo_half, zero, wk_bf), v_ref[2 * pr + 1], (((0,), (0,)), ((), ())),
                                 preferred_element_type=F32))
        row = lax.broadcasted_iota(jnp.int32, (LANES, LANES), 0)
        c_ref[pr] = jnp.where(row < QK_DIM, a_vals[0], a_vals[1]) * c_pair + upd
        lane1 = lax.broadcasted_iota(jnp.int32, (1, LANES), 1)
        n_ref[pr] = (jnp.where(lane1 < QK_DIM, a_vals[0], a_vals[1]) * n_pair_row
                     + jnp.sum(wk, axis=0, keepdims=True))


def _mlstm(p, gates, gate_bias, conv_w, conv_b, norm_g, *, bsz, seq, chunk):
    m = bsz * seq
    nc = seq // chunk

    def slab_map(group):
        return lambda b, c: (group, b * nc + c, 0)

    return pl.pallas_call(
        functools.partial(_mlstm_kernel, chunk=chunk),
        grid=(bsz, nc),
        in_specs=[
            pl.BlockSpec((HEADS, chunk, LANES), slab_map(SLAB_BQK // HEADS)),
            pl.BlockSpec((HEADS, chunk, LANES), slab_map(SLAB_BV // HEADS)),
            pl.BlockSpec((HEADS, chunk, LANES), slab_map(SLAB_BO // HEADS)),
            pl.BlockSpec((HEADS, chunk, LANES), slab_map(SLAB_BZ // HEADS)),
            pl.BlockSpec((chunk, LANES), lambda b, c: (b * nc + c, 0)),
            pl.BlockSpec((1, LANES), lambda b, c: (0, 0)),
            pl.BlockSpec((CONV_WIDTH * 8, LANES), lambda b, c: (0, 0)),
            pl.BlockSpec((8, LANES), lambda b, c: (0, 0)),
            pl.BlockSpec((1, V_DIM), lambda b, c: (0, 0)),
        ],
        out_specs=pl.BlockSpec((HEADS, chunk, LANES), lambda b, c: (0, b * nc + c, 0)),
        out_shape=jax.ShapeDtypeStruct((HEADS, m, LANES), BF16),
        scratch_shapes=[
            pltpu.VMEM((HEADS, SUBLANES, LANES), F32),
            pltpu.VMEM((HEADS // 2, LANES, V_DIM), F32),
            pltpu.VMEM((HEADS // 2, 1, LANES), F32),
            pltpu.VMEM((HEADS, 1, 1), F32),
        ],
        compiler_params=pltpu.CompilerParams(
            dimension_semantics=("arbitrary", "arbitrary"), vmem_limit_bytes=VMEM_LIMIT),
        name="mlstm",
    )(p, p, p, p, gates, gate_bias, conv_w, conv_b, norm_g)


def _outproj_kernel(oa_ref, hb_ref, ga_ref, gb_ref, x_ref, wa_ref, wb_ref, wo_ref, o_ref):
    def cat(ref):
        return jnp.concatenate([ref[h] for h in range(HEADS)], axis=1)

    ya = jnp.dot(cat(oa_ref), wa_ref[...], preferred_element_type=F32)
    yb = jnp.dot(cat(hb_ref), wb_ref[...], preferred_element_type=F32)
    u = _sigmoid(cat(ga_ref).astype(F32)) * ya + _sigmoid(cat(gb_ref).astype(F32)) * yb
    o_ref[...] = x_ref[...] + jnp.dot(u.astype(BF16), wo_ref[...], preferred_element_type=F32)


def _outproj(oa, hb, p, x2, wa, wb, wo, *, tm):
    m = x2.shape[0]
    w_spec = pl.BlockSpec((D_MODEL, D_MODEL), lambda i: (0, 0))
    return pl.pallas_call(
        _outproj_kernel,
        grid=(m // tm,),
        in_specs=[
            pl.BlockSpec((HEADS, tm, LANES), lambda i: (0, i, 0)),
            pl.BlockSpec((HEADS, tm, LANES), lambda i: (0, i, 0)),
            pl.BlockSpec((HEADS, tm, LANES), lambda i: (SLAB_GA // HEADS, i, 0)),
            pl.BlockSpec((HEADS, tm, LANES), lambda i: (SLAB_GB // HEADS, i, 0)),
            pl.BlockSpec((tm, D_MODEL), lambda i: (i, 0)),
            w_spec, w_spec, w_spec,
        ],
        out_specs=pl.BlockSpec((tm, D_MODEL), lambda i: (i, 0)),
        out_shape=jax.ShapeDtypeStruct((m, D_MODEL), F32),
        compiler_params=pltpu.CompilerParams(
            dimension_semantics=("arbitrary",), vmem_limit_bytes=VMEM_LIMIT),
        name="outproj",
    )(oa, hb, p, p, x2, wa, wb, wo)


def _reorder_w_in(w_in):
    idx = [0]
    for s in _REF_SPLITS:
        idx.append(idx[-1] + s)
    parts = [w_in[:, idx[i]:idx[i + 1]] for i in range(len(_REF_SPLITS))]
    aq, ak, av, az, bqk, bv, bi, bf, bo, bz, ga, gb = parts
    w_main = jnp.concatenate([aq, ak, av, az, bqk, bv, bo, bz, ga, gb], axis=1).astype(BF16)
    w_gate = jnp.concatenate([bi, bf, jnp.zeros((D_MODEL, LANES - 2 * HEADS), w_in.dtype)], axis=1).astype(BF16)
    return w_main, w_gate


def _pick(n, pref):
    return pref if n % pref == 0 else n


def kernel(x, positions, norm_g, w_in, q_norm_g, k_norm_g, lambda_qk, attn_norm_g, w_out_a,
           conv_w, conv_b, igate_b, fgate_b, mlstm_norm_g, w_out_b, w_o):
    bsz, seq, _ = x.shape
    m = bsz * seq
    depth = w_in.shape[0]
    tm = _pick(m, 1024)
    blk = _pick(seq, 512)
    chunk = _pick(seq, 256)

    inv_freq = ROPE_THETA ** (-jnp.arange(0, QK_DIM, 2, dtype=F32) / QK_DIM)
    ang = positions.astype(F32).reshape(m, 1) * inv_freq[None, :]
    cos, sin = jnp.cos(ang), jnp.sin(ang)
    cos4 = jnp.concatenate([cos, cos, cos, cos], axis=1)
    sin4 = jnp.concatenate([-sin, sin, -sin, sin], axis=1)
    grp = jnp.arange(LANES) // QK_DIM
    ones_blk = (grp[:, None] == grp[None, :]).astype(BF16)

    x2 = x.reshape(m, D_MODEL)
    for l in range(depth):
        lam_init = 0.8 - 0.6 * math.exp(-0.3 * l)
        w_main, w_gate = _reorder_w_in(w_in[l])
        p, gates = _inproj(x2, norm_g[l][None, :], w_main, w_gate, tm=tm, tn=1024)

        gq = jnp.tile(q_norm_g[l], 2) * (QK_DIM ** -0.5)
        gk = jnp.tile(k_norm_g[l], 2)
        gains = jnp.concatenate([jnp.broadcast_to(gq, (HEADS, LANES)),
                                 jnp.broadcast_to(gk, (HEADS, LANES))], axis=0)[:, None, :]
        qk = _qkprep(p, cos4, sin4, gains, ones_blk, tm=tm)
        oa = _attention(qk, p, lambda_qk[l], attn_norm_g[l][None, :],
                        bsz=bsz, seq=seq, blk=blk, lam_init=lam_init)

        gate_bias = jnp.concatenate([igate_b[l], fgate_b[l], jnp.zeros((LANES - 2 * HEADS,), F32)])[None, :]
        cw = conv_w[l].reshape(CONV_WIDTH * 8, LANES)
        cb = conv_b[l].reshape(8, LANES)
        hb = _mlstm(p, gates, gate_bias, cw, cb, mlstm_norm_g[l][None, :], bsz=bsz, seq=seq, chunk=chunk)

        x2 = _outproj(oa, hb, p, x2, w_out_a[l].astype(BF16), w_out_b[l].astype(BF16),
                      w_o[l].astype(BF16), tm=_pick(m, 512))
    return x2.reshape(bsz, seq, D_MODEL)
```

```python
import functools
import math

import jax
import jax.numpy as jnp
from jax import lax
from jax.experimental import pallas as pl
from jax.experimental.pallas import tpu as pltpu

F32 = jnp.float32
BF16 = jnp.bfloat16

D_MODEL = 1024
HEADS = 8
QK_DIM = 64
V_DIM = 128
CONV_WIDTH = 4
ROPE_THETA = 10000.0
EPS = 1e-6
NEG_INIT = -1e30
LOG2_E = math.log2(math.e)
LANES = 128
SUBLANES = 8
VMEM_LIMIT = 48 * 1024 * 1024

SLAB_AQ, SLAB_AK, SLAB_AV, SLAB_AZ = 0, 8, 16, 24
SLAB_BQK, SLAB_BV, SLAB_BO, SLAB_BZ, SLAB_GA, SLAB_GB = 32, 40, 48, 56, 64, 72
N_SLABS = 80
N_MAIN = N_SLABS * LANES

_REF_SPLITS = (1024, 1024, 1024, 1024, 1024, 1024, 8, 8, 1024, 1024, 1024, 1024)


def _split_hi_lo(x):
    hi = x.astype(BF16)
    lo = (x - hi.astype(F32)).astype(BF16)
    return hi, lo


def _sigmoid(x):
    return 1.0 / (1.0 + jnp.exp(-x))


def _inproj_kernel(x_ref, g_ref, w_ref, wg_ref, p_ref, gate_ref, h_ref, *, n_sub):
    @pl.when(pl.program_id(1) == 0)
    def _():
        x = x_ref[...]
        ms = jnp.mean(x * x, axis=-1, keepdims=True)
        h = ((x * lax.rsqrt(ms + EPS)) * g_ref[...]).astype(BF16)
        h_ref[...] = h
        gate_ref[...] = jnp.dot(h, wg_ref[...], preferred_element_type=F32)

    acc = jnp.dot(h_ref[...], w_ref[...], preferred_element_type=F32)
    for c in range(n_sub):
        p_ref[c] = acc[:, c * LANES:(c + 1) * LANES].astype(BF16)


def _inproj(x2, norm_g, w_main, w_gate, *, tm, tn):
    m = x2.shape[0]
    n_sub = tn // LANES
    return pl.pallas_call(
        functools.partial(_inproj_kernel, n_sub=n_sub),
        grid=(m // tm, N_MAIN // tn),
        in_specs=[
            pl.BlockSpec((tm, D_MODEL), lambda i, j: (i, 0)),
            pl.BlockSpec((1, D_MODEL), lambda i, j: (0, 0)),
            pl.BlockSpec((D_MODEL, tn), lambda i, j: (0, j)),
            pl.BlockSpec((D_MODEL, LANES), lambda i, j: (0, 0)),
        ],
        out_specs=[
            pl.BlockSpec((n_sub, tm, LANES), lambda i, j: (j, i, 0)),
            pl.BlockSpec((tm, LANES), lambda i, j: (i, 0)),
        ],
        out_shape=[
            jax.ShapeDtypeStruct((N_SLABS, m, LANES), BF16),
            jax.ShapeDtypeStruct((m, LANES), F32),
        ],
        scratch_shapes=[pltpu.VMEM((tm, D_MODEL), BF16)],
        compiler_params=pltpu.CompilerParams(
            dimension_semantics=("arbitrary", "arbitrary"), vmem_limit_bytes=VMEM_LIMIT),
        name="inproj",
    )(x2, norm_g, w_main, w_gate)


def _qkprep_kernel(p_ref, cos_ref, sin_ref, g_ref, ones_ref, o_ref):
    x = p_ref[...].astype(F32)
    hi, lo = _split_hi_lo(x * x)
    ones = ones_ref[...]
    ss = (jnp.dot(hi, ones, preferred_element_type=F32)
          + jnp.dot(lo, ones, preferred_element_type=F32))
    y = (x * lax.rsqrt(ss * (1.0 / QK_DIM) + EPS)) * g_ref[...]
    lane = lax.broadcasted_iota(jnp.int32, y.shape, 1)
    half = QK_DIM // 2
    partner = jnp.where((lane & half) == 0,
                        pltpu.roll(y, LANES - half, axis=1),
                        pltpu.roll(y, half, axis=1))
    o_ref[...] = (y * cos_ref[...] + partner * sin_ref[...]).astype(BF16)


def _qkprep(p, cos4, sin4, gains, ones_blk, *, tm):
    m = p.shape[1]
    n_qk = 2 * HEADS
    return pl.pallas_call(
        _qkprep_kernel,
        grid=(m // tm, n_qk),
        in_specs=[
            pl.BlockSpec((None, tm, LANES), lambda i, c: (c, i, 0)),
            pl.BlockSpec((tm, LANES), lambda i, c: (i, 0)),
            pl.BlockSpec((tm, LANES), lambda i, c: (i, 0)),
            pl.BlockSpec((None, 1, LANES), lambda i, c: (c, 0, 0)),
            pl.BlockSpec((LANES, LANES), lambda i, c: (0, 0)),
        ],
        out_specs=pl.BlockSpec((None, tm, LANES), lambda i, c: (c, i, 0)),
        out_shape=jax.ShapeDtypeStruct((n_qk, m, LANES), BF16),
        compiler_params=pltpu.CompilerParams(
            dimension_semantics=("arbitrary", "arbitrary"), vmem_limit_bytes=VMEM_LIMIT),
        name="qkprep",
    )(p, cos4, sin4, gains, ones_blk)


def _vprep_kernel(v_ref, o_ref, *, blk, n_sub):
    for c in range(n_sub):
        o_ref[c] = jnp.transpose(v_ref[c * blk:(c + 1) * blk, :].astype(F32)).astype(BF16)


def _vprep(p, *, tm, blk):
    m = p.shape[1]
    n_sub = tm // blk
    return pl.pallas_call(
        functools.partial(_vprep_kernel, blk=blk, n_sub=n_sub),
        grid=(HEADS, m // tm),
        in_specs=[pl.BlockSpec((None, tm, LANES), lambda h, i: (SLAB_AV + h, i, 0))],
        out_specs=pl.BlockSpec((None, n_sub, V_DIM, blk), lambda h, i: (h, i, 0, 0)),
        out_shape=jax.ShapeDtypeStruct((HEADS, m // blk, V_DIM, blk), BF16),
        compiler_params=pltpu.CompilerParams(
            dimension_semantics=("arbitrary", "arbitrary"), vmem_limit_bytes=VMEM_LIMIT),
        name="vprep",
    )(p)


def _attn_kernel(q_ref, k_ref, vt_ref, az_ref, lam_ref, g_ref, o_ref, qs_ref, acc_ref, *, blk, lam_init):
    qi = pl.program_id(1)
    q = q_ref[...]
    lane = lax.broadcasted_iota(jnp.int32, q.shape, 1)
    zero = jnp.zeros_like(q)
    qs_ref[0:blk, :] = jnp.where(lane < QK_DIM, q, zero)
    qs_ref[blk:2 * blk, :] = jnp.where(lane >= QK_DIM, q, zero)
    acc_ref[...] = jnp.zeros(acc_ref.shape, F32)

    def step(kc, m, l, masked):
        k = k_ref[pl.ds(pl.multiple_of(kc * blk, blk), blk), :]
        s = lax.dot_general(k, qs_ref[...], (((1,), (1,)), ((), ())),
                            preferred_element_type=F32)
        if masked:
            kpos = lax.broadcasted_iota(jnp.int32, s.shape, 0)
            qpos = lax.broadcasted_iota(jnp.int32, s.shape, 1) & (blk - 1)
            s = jnp.where(kpos <= qpos, s, -jnp.inf)
        m_new = jnp.maximum(m, jnp.max(s, axis=0, keepdims=True))
        alpha = jnp.exp2(m - m_new)
        p = jnp.exp2(s - m_new)
        l_new = alpha * l + jnp.sum(p, axis=0, keepdims=True)
        pv = jnp.dot(vt_ref[kc], p.astype(BF16), preferred_element_type=F32)
        acc_ref[...] = acc_ref[...] * alpha + pv
        return m_new, l_new

    m0 = jnp.full((1, 2 * blk), -jnp.inf, F32)
    l0 = jnp.zeros((1, 2 * blk), F32)
    m, l = lax.fori_loop(0, qi, lambda kc, c: step(kc, c[0], c[1], False), (m0, l0))
    m, l = step(qi, m, l, True)

    lq = lam_ref[...]
    lam = (jnp.exp(jnp.sum(lq[0:1] * lq[1:2], axis=-1, keepdims=True))
           - jnp.exp(jnp.sum(lq[2:3] * lq[3:4], axis=-1, keepdims=True)) + lam_init)
    o_t = acc_ref[...] / l
    o = jnp.transpose(o_t[:, 0:blk] - lam * o_t[:, blk:2 * blk])
    ms = jnp.mean(o * o, axis=-1, keepdims=True)
    y = (o * lax.rsqrt(ms + EPS)) * g_ref[...] * (1.0 - lam_init)
    az = az_ref[...].astype(F32)
    o_ref[...] = (y * (az * _sigmoid(az))).astype(BF16)


def _attention(qk, vt, p, lambda_qk, attn_norm_g, *, bsz, seq, blk, lam_init):
    m = bsz * seq
    nb = seq // blk

    def q_map(bh, qi):
        return (bh % HEADS, (bh // HEADS) * nb + qi, 0)

    return pl.pallas_call(
        functools.partial(_attn_kernel, blk=blk, lam_init=lam_init),
        grid=(bsz * HEADS, nb),
        in_specs=[
            pl.BlockSpec((None, blk, LANES), q_map),
            pl.BlockSpec((None, seq, LANES), lambda bh, qi: (HEADS + bh % HEADS, bh // HEADS, 0)),
            pl.BlockSpec((None, nb, V_DIM, blk), lambda bh, qi: (bh % HEADS, bh // HEADS, 0, 0)),
            pl.BlockSpec((None, blk, LANES), lambda bh, qi: (SLAB_AZ + bh % HEADS, (bh // HEADS) * nb + qi, 0)),
            pl.BlockSpec((4, QK_DIM), lambda bh, qi: (0, 0)),
            pl.BlockSpec((1, V_DIM), lambda bh, qi: (0, 0)),
        ],
        out_specs=pl.BlockSpec((None, blk, LANES), q_map),
        out_shape=jax.ShapeDtypeStruct((HEADS, m, LANES), BF16),
        scratch_shapes=[
            pltpu.VMEM((2 * blk, LANES), BF16),
            pltpu.VMEM((V_DIM, 2 * blk), F32),
        ],
        compiler_params=pltpu.CompilerParams(
            dimension_semantics=("arbitrary", "arbitrary"), vmem_limit_bytes=VMEM_LIMIT),
        name="diffattn",
    )(qk, qk, vt, p, lambda_qk, attn_norm_g)


def _log_sigmoid(x):
    return jnp.minimum(x, 0.0) - jnp.log(1.0 + jnp.exp(-jnp.abs(x)))


def _mlstm_kernel(qk_ref, v_ref, bo_ref, bz_ref, gate_ref, gbias_ref, cw_ref, cb_ref, ng_ref,
                  o_ref, tail_ref, c_ref, n_ref, m_ref, *, chunk):
    L = chunk
    n_pair = HEADS // 2

    @pl.when(pl.program_id(1) == 0)
    def _():
        tail_ref[...] = jnp.zeros(tail_ref.shape, F32)
        c_ref[...] = jnp.zeros(c_ref.shape, F32)
        n_ref[...] = jnp.zeros(n_ref.shape, F32)
        m_ref[...] = jnp.full(m_ref.shape, NEG_INIT, F32)

    row8 = lax.broadcasted_iota(jnp.int32, (SUBLANES, LANES), 0)
    conv = []
    for sl in range(2 * n_pair):
        x = qk_ref[sl].astype(F32)
        tail = tail_ref[sl]
        y = x * cw_ref[(CONV_WIDTH - 1) * 8 + sl: (CONV_WIDTH - 1) * 8 + sl + 1, :] + cb_ref[sl:sl + 1, :]
        for d in range(1, CONV_WIDTH):
            xr = pltpu.roll(x, d, axis=0)
            tr = pltpu.roll(tail, d, axis=0)
            first = jnp.where(row8 < d, tr, xr[0:SUBLANES])
            xs = jnp.concatenate([first, xr[SUBLANES:]], axis=0)
            j = CONV_WIDTH - 1 - d
            y = y + xs * cw_ref[j * 8 + sl: j * 8 + sl + 1, :]
        tail_ref[sl] = x[L - SUBLANES:L]
        conv.append(y * _sigmoid(y))
    lane = lax.broadcasted_iota(jnp.int32, (L, LANES), 1)
    lo_half = lane < QK_DIM

    g = gate_ref[...] + gbias_ref[...]
    ls = _log_sigmoid(g)
    r_i = lax.broadcasted_iota(jnp.int32, (L, L), 0)
    c_i = lax.broadcasted_iota(jnp.int32, (L, L), 1)
    causal = c_i <= r_i
    tri = jnp.where(causal, 1.0, 0.0).astype(BF16)
    ls_hi, ls_lo = _split_hi_lo(ls)
    bcum = (jnp.dot(tri, ls_hi, preferred_element_type=F32)
            + jnp.dot(tri, ls_lo, preferred_element_type=F32))
    g_t = jnp.transpose(g)
    b_t = jnp.transpose(bcum)

    for pr in range(n_pair):
        q_slab = conv[pr] * (QK_DIM ** -0.5)
        k_slab = conv[n_pair + pr]
        q_bf = q_slab.astype(BF16)
        zero = jnp.zeros_like(q_bf)
        q_stack = jnp.concatenate([jnp.where(lo_half, q_bf, zero), jnp.where(lo_half, zero, q_bf)], axis=0)
        k_bf = k_slab.astype(BF16)
        s_pair = lax.dot_general(q_stack, k_bf, (((1,), (1,)), ((), ())), preferred_element_type=F32)
        c_pair = c_ref[pr]
        qc_pair = jnp.dot(q_stack, c_pair.astype(BF16), preferred_element_type=F32)
        n_pair_row = n_ref[pr]
        qn = q_slab * n_pair_row

        w_cols = []
        a_vals = []
        for half in range(2):
            h = 2 * pr + half
            bcol = bcum[:, HEADS + h: HEADS + h + 1]
            icol = g[:, h: h + 1]
            brow = b_t[HEADS + h: HEADS + h + 1, :]
            irow = g_t[h: h + 1, :]
            blast = brow[:, L - 1: L]
            m_prev = m_ref[h]

            dmat = jnp.where(causal, bcol + (irow - brow), -jnp.inf)
            inter_log = bcol + m_prev
            m_t = jnp.maximum(inter_log, jnp.max(dmat, axis=-1, keepdims=True))
            sc = s_pair[half * L:(half + 1) * L] * jnp.exp(dmat - m_t)
            inter_w = jnp.exp(inter_log - m_t)
            v_h = v_ref[h]
            num = (jnp.dot(sc.astype(BF16), v_h, preferred_element_type=F32)
                   + inter_w * qc_pair[half * L:(half + 1) * L])
            qn_h = jnp.sum(jnp.where(lo_half if half == 0 else ~lo_half, qn, 0.0), axis=-1, keepdims=True)
            den = jnp.sum(sc, axis=-1, keepdims=True) + inter_w * qn_h
            hh = num / jnp.maximum(jnp.abs(den), jnp.exp(-m_t))

            ms = jnp.mean(hh * hh, axis=-1, keepdims=True)
            y = (hh * lax.rsqrt(ms + EPS)) * ng_ref[...]
            bo = bo_ref[h].astype(F32)
            bz = bz_ref[h].astype(F32)
            o_ref[h] = (_sigmoid(bo) * y * (bz * _sigmoid(bz))).astype(BF16)

            m_loc = jnp.max(blast - brow + irow, axis=-1, keepdims=True)
            m_new = jnp.maximum(blast + m_prev, m_loc)
            a_vals.append(jnp.exp(blast + m_prev - m_new))
            w_cols.append(jnp.exp((blast - m_new) - bcol + icol))
            m_ref[h] = m_new

        wk = k_slab * jnp.where(lo_half, w_cols[0], w_cols[1])
        wk_bf = wk.astype(BF16)
        zero = jnp.zeros_like(wk_bf)
        upd = (lax.dot_general(jnp.where(lo_half, wk_bf, zero), v_ref[2 * pr], (((0,), (0,)), ((), ())),
                               preferred_element_type=F32)
               + lax.dot_general(jnp.where(lo_half, zero, wk_bf), v_ref[2 * pr + 1], (((0,), (0,)), ((), ())),
                                 preferred_element_type=F32))
        row = lax.broadcasted_iota(jnp.int32, (LANES, LANES), 0)
        c_ref[pr] = jnp.where(row < QK_DIM, a_vals[0], a_vals[1]) * c_pair + upd
        lane1 = lax.broadcasted_iota(jnp.int32, (1, LANES), 1)
        n_ref[pr] = (jnp.where(lane1 < QK_DIM, a_vals[0], a_vals[1]) * n_pair_row
                     + jnp.sum(wk, axis=0, keepdims=True))


def _mlstm(p, gates, gate_bias, conv_w, conv_b, norm_g, *, bsz, seq, chunk):
    m = bsz * seq
    nc = seq // chunk

    def slab_map(group):
        return lambda b, c: (group, b * nc + c, 0)

    return pl.pallas_call(
        functools.partial(_mlstm_kernel, chunk=chunk),
        grid=(bsz, nc),
        in_specs=[
            pl.BlockSpec((HEADS, chunk, LANES), slab_map(SLAB_BQK // HEADS)),
            pl.BlockSpec((HEADS, chunk, LANES), slab_map(SLAB_BV // HEADS)),
            pl.BlockSpec((HEADS, chunk, LANES), slab_map(SLAB_BO // HEADS)),
            pl.BlockSpec((HEADS, chunk, LANES), slab_map(SLAB_BZ // HEADS)),
            pl.BlockSpec((chunk, LANES), lambda b, c: (b * nc + c, 0)),
            pl.BlockSpec((1, LANES), lambda b, c: (0, 0)),
            pl.BlockSpec((CONV_WIDTH * 8, LANES), lambda b, c: (0, 0)),
            pl.BlockSpec((8, LANES), lambda b, c: (0, 0)),
            pl.BlockSpec((1, V_DIM), lambda b, c: (0, 0)),
        ],
        out_specs=pl.BlockSpec((HEADS, chunk, LANES), lambda b, c: (0, b * nc + c, 0)),
        out_shape=jax.ShapeDtypeStruct((HEADS, m, LANES), BF16),
        scratch_shapes=[
            pltpu.VMEM((HEADS, SUBLANES, LANES), F32),
            pltpu.VMEM((HEADS // 2, LANES, V_DIM), F32),
            pltpu.VMEM((HEADS // 2, 1, LANES), F32),
            pltpu.VMEM((HEADS, 1, 1), F32),
        ],
        compiler_params=pltpu.CompilerParams(
            dimension_semantics=("arbitrary", "arbitrary"), vmem_limit_bytes=VMEM_LIMIT),
        name="mlstm",
    )(p, p, p, p, gates, gate_bias, conv_w, conv_b, norm_g)


def _outproj_kernel(oa_ref, hb_ref, ga_ref, gb_ref, x_ref, wa_ref, wb_ref, wo_ref, o_ref):
    def cat(ref):
        return jnp.concatenate([ref[h] for h in range(HEADS)], axis=1)

    ya = jnp.dot(cat(oa_ref), wa_ref[...], preferred_element_type=F32)
    yb = jnp.dot(cat(hb_ref), wb_ref[...], preferred_element_type=F32)
    u = _sigmoid(cat(ga_ref).astype(F32)) * ya + _sigmoid(cat(gb_ref).astype(F32)) * yb
    o_ref[...] = x_ref[...] + jnp.dot(u.astype(BF16), wo_ref[...], preferred_element_type=F32)


def _outproj(oa, hb, p, x2, wa, wb, wo, *, tm):
    m = x2.shape[0]
    w_spec = pl.BlockSpec((D_MODEL, D_MODEL), lambda i: (0, 0))
    return pl.pallas_call(
        _outproj_kernel,
        grid=(m // tm,),
        in_specs=[
            pl.BlockSpec((HEADS, tm, LANES), lambda i: (0, i, 0)),
            pl.BlockSpec((HEADS, tm, LANES), lambda i: (0, i, 0)),
            pl.BlockSpec((HEADS, tm, LANES), lambda i: (SLAB_GA // HEADS, i, 0)),
            pl.BlockSpec((HEADS, tm, LANES), lambda i: (SLAB_GB // HEADS, i, 0)),
            pl.BlockSpec((tm, D_MODEL), lambda i: (i, 0)),
            w_spec, w_spec, w_spec,
        ],
        out_specs=pl.BlockSpec((tm, D_MODEL), lambda i: (i, 0)),
        out_shape=jax.ShapeDtypeStruct((m, D_MODEL), F32),
        compiler_params=pltpu.CompilerParams(
            dimension_semantics=("arbitrary",), vmem_limit_bytes=VMEM_LIMIT),
        name="outproj",
    )(oa, hb, p, p, x2, wa, wb, wo)


def _reorder_w_in(w_in):
    idx = [0]
    for s in _REF_SPLITS:
        idx.append(idx[-1] + s)
    parts = [w_in[:, idx[i]:idx[i + 1]] for i in range(len(_REF_SPLITS))]
    aq, ak, av, az, bqk, bv, bi, bf, bo, bz, ga, gb = parts
    w_main = jnp.concatenate([aq, ak, av, az, bqk, bv, bo, bz, ga, gb], axis=1).astype(BF16)
    w_gate = jnp.concatenate([bi, bf, jnp.zeros((D_MODEL, LANES - 2 * HEADS), w_in.dtype)], axis=1).astype(BF16)
    return w_main, w_gate


def _pick(n, pref):
    return pref if n % pref == 0 else n


def kernel(x, positions, norm_g, w_in, q_norm_g, k_norm_g, lambda_qk, attn_norm_g, w_out_a,
           conv_w, conv_b, igate_b, fgate_b, mlstm_norm_g, w_out_b, w_o):
    bsz, seq, _ = x.shape
    m = bsz * seq
    depth = w_in.shape[0]
    tm = _pick(m, 1024)
    blk = _pick(seq, 512)
    chunk = _pick(seq, 256)

    inv_freq = ROPE_THETA ** (-jnp.arange(0, QK_DIM, 2, dtype=F32) / QK_DIM)
    ang = positions.astype(F32).reshape(m, 1) * inv_freq[None, :]
    cos, sin = jnp.cos(ang), jnp.sin(ang)
    cos4 = jnp.concatenate([cos, cos, cos, cos], axis=1)
    sin4 = jnp.concatenate([-sin, sin, -sin, sin], axis=1)
    grp = jnp.arange(LANES) // QK_DIM
    ones_blk = (grp[:, None] == grp[None, :]).astype(BF16)

    x2 = x.reshape(m, D_MODEL)
    for l in range(depth):
        lam_init = 0.8 - 0.6 * math.exp(-0.3 * l)
        w_main, w_gate = _reorder_w_in(w_in[l])
        p, gates = _inproj(x2, norm_g[l][None, :], w_main, w_gate, tm=tm, tn=1024)

        gq = jnp.tile(q_norm_g[l], 2) * (QK_DIM ** -0.5 * LOG2_E)
        gk = jnp.tile(k_norm_g[l], 2)
        gains = jnp.concatenate([jnp.broadcast_to(gq, (HEADS, LANES)),
                                 jnp.broadcast_to(gk, (HEADS, LANES))], axis=0)[:, None, :]
        qk = _qkprep(p, cos4, sin4, gains, ones_blk, tm=tm)
        vt = _vprep(p, tm=tm, blk=blk)
        oa = _attention(qk, vt, p, lambda_qk[l], attn_norm_g[l][None, :],
                        bsz=bsz, seq=seq, blk=blk, lam_init=lam_init)

        gate_bias = jnp.concatenate([igate_b[l], fgate_b[l], jnp.zeros((LANES - 2 * HEADS,), F32)])[None, :]
        cw = conv_w[l].reshape(CONV_WIDTH * 8, LANES)
        cb = conv_b[l].reshape(8, LANES)
        hb = _mlstm(p, gates, gate_bias, cw, cb, mlstm_norm_g[l][None, :], bsz=bsz, seq=seq, chunk=chunk)

        x2 = _outproj(oa, hb, p, x2, w_out_a[l].astype(BF16), w_out_b[l].astype(BF16),
                      w_o[l].astype(BF16), tm=_pick(m, 512))
    return x2.reshape(bsz, seq, D_MODEL)
```

```python
import functools
import math

import jax
import jax.numpy as jnp
from jax import lax
from jax.experimental import pallas as pl
from jax.experimental.pallas import tpu as pltpu

F32 = jnp.float32
BF16 = jnp.bfloat16

D_MODEL = 1024
HEADS = 8
QK_DIM = 64
V_DIM = 128
CONV_WIDTH = 4
ROPE_THETA = 10000.0
EPS = 1e-6
NEG_INIT = -1e30
LOG2_E = math.log2(math.e)
LANES = 128
SUBLANES = 8
VMEM_LIMIT = 48 * 1024 * 1024

SLAB_AQ, SLAB_AK, SLAB_AV, SLAB_AZ = 0, 8, 16, 24
SLAB_BQK, SLAB_BV, SLAB_BO, SLAB_BZ, SLAB_GA, SLAB_GB = 32, 40, 48, 56, 64, 72
N_SLABS = 80
N_MAIN = N_SLABS * LANES

_REF_SPLITS = (1024, 1024, 1024, 1024, 1024, 1024, 8, 8, 1024, 1024, 1024, 1024)


def _split_hi_lo(x):
    hi = x.astype(BF16)
    lo = (x - hi.astype(F32)).astype(BF16)
    return hi, lo


def _sigmoid(x):
    return 1.0 / (1.0 + jnp.exp(-x))


def _inproj_kernel(x_ref, g_ref, w_ref, wg_ref, p_ref, gate_ref, h_ref, *, n_sub):
    @pl.when(pl.program_id(1) == 0)
    def _():
        x = x_ref[...]
        ms = jnp.mean(x * x, axis=-1, keepdims=True)
        h = ((x * lax.rsqrt(ms + EPS)) * g_ref[...]).astype(BF16)
        h_ref[...] = h
        gate_ref[...] = jnp.dot(h, wg_ref[...], preferred_element_type=F32)

    acc = jnp.dot(h_ref[...], w_ref[...], preferred_element_type=F32)
    for c in range(n_sub):
        p_ref[c] = acc[:, c * LANES:(c + 1) * LANES].astype(BF16)


def _inproj(x2, norm_g, w_main, w_gate, *, tm, tn):
    m = x2.shape[0]
    n_sub = tn // LANES
    return pl.pallas_call(
        functools.partial(_inproj_kernel, n_sub=n_sub),
        grid=(m // tm, N_MAIN // tn),
        in_specs=[
            pl.BlockSpec((tm, D_MODEL), lambda i, j: (i, 0)),
            pl.BlockSpec((1, D_MODEL), lambda i, j: (0, 0)),
            pl.BlockSpec((D_MODEL, tn), lambda i, j: (0, j)),
            pl.BlockSpec((D_MODEL, LANES), lambda i, j: (0, 0)),
        ],
        out_specs=[
            pl.BlockSpec((n_sub, tm, LANES), lambda i, j: (j, i, 0)),
            pl.BlockSpec((tm, LANES), lambda i, j: (i, 0)),
        ],
        out_shape=[
            jax.ShapeDtypeStruct((N_SLABS, m, LANES), BF16),
            jax.ShapeDtypeStruct((m, LANES), F32),
        ],
        scratch_shapes=[pltpu.VMEM((tm, D_MODEL), BF16)],
        compiler_params=pltpu.CompilerParams(
            dimension_semantics=("arbitrary", "arbitrary"), vmem_limit_bytes=VMEM_LIMIT),
        name="inproj",
    )(x2, norm_g, w_main, w_gate)


def _qkprep_kernel(p_ref, cos_ref, sin_ref, g_ref, ones_ref, o_ref):
    x = p_ref[...].astype(F32)
    hi, lo = _split_hi_lo(x * x)
    ones = ones_ref[...]
    ss = (jnp.dot(hi, ones, preferred_element_type=F32)
          + jnp.dot(lo, ones, preferred_element_type=F32))
    y = (x * lax.rsqrt(ss * (1.0 / QK_DIM) + EPS)) * g_ref[...]
    lane = lax.broadcasted_iota(jnp.int32, y.shape, 1)
    half = QK_DIM // 2
    partner = jnp.where((lane & half) == 0,
                        pltpu.roll(y, LANES - half, axis=1),
                        pltpu.roll(y, half, axis=1))
    o_ref[...] = (y * cos_ref[...] + partner * sin_ref[...]).astype(BF16)


def _qkprep(p, cos4, sin4, gains, ones_blk, *, tm):
    m = p.shape[1]
    n_qk = 2 * HEADS
    return pl.pallas_call(
        _qkprep_kernel,
        grid=(m // tm, n_qk),
        in_specs=[
            pl.BlockSpec((None, tm, LANES), lambda i, c: (c, i, 0)),
            pl.BlockSpec((tm, LANES), lambda i, c: (i, 0)),
            pl.BlockSpec((tm, LANES), lambda i, c: (i, 0)),
            pl.BlockSpec((None, 1, LANES), lambda i, c: (c, 0, 0)),
            pl.BlockSpec((LANES, LANES), lambda i, c: (0, 0)),
        ],
        out_specs=pl.BlockSpec((None, tm, LANES), lambda i, c: (c, i, 0)),
        out_shape=jax.ShapeDtypeStruct((n_qk, m, LANES), BF16),
        compiler_params=pltpu.CompilerParams(
            dimension_semantics=("arbitrary", "arbitrary"), vmem_limit_bytes=VMEM_LIMIT),
        name="qkprep",
    )(p, cos4, sin4, gains, ones_blk)


def _vprep_kernel(v_ref, o_ref, *, blk, n_sub):
    for c in range(n_sub):
        o_ref[c] = jnp.transpose(v_ref[c * blk:(c + 1) * blk, :].astype(F32)).astype(BF16)


def _vprep(p, *, tm, blk):
    m = p.shape[1]
    n_sub = tm // blk
    return pl.pallas_call(
        functools.partial(_vprep_kernel, blk=blk, n_sub=n_sub),
        grid=(HEADS, m // tm),
        in_specs=[pl.BlockSpec((None, tm, LANES), lambda h, i: (SLAB_AV + h, i, 0))],
        out_specs=pl.BlockSpec((None, n_sub, V_DIM, blk), lambda h, i: (h, i, 0, 0)),
        out_shape=jax.ShapeDtypeStruct((HEADS, m // blk, V_DIM, blk), BF16),
        compiler_params=pltpu.CompilerParams(
            dimension_semantics=("arbitrary", "arbitrary"), vmem_limit_bytes=VMEM_LIMIT),
        name="vprep",
    )(p)


def _attn_kernel(q_ref, k_ref, vt_ref, az_ref, lam_ref, g_ref, o_ref,
                 qs_ref, sa_ref, sb_ref, ca_ref, cb_ref, m_ref, l_ref, acc_ref, *, blk, lam_init):
    qi = pl.program_id(1)
    q = q_ref[...]
    lane = lax.broadcasted_iota(jnp.int32, q.shape, 1)
    zero = jnp.zeros_like(q)
    qs_ref[0:blk, :] = jnp.where(lane < QK_DIM, q, zero)
    qs_ref[blk:2 * blk, :] = jnp.where(lane >= QK_DIM, q, zero)
    acc_ref[...] = jnp.zeros(acc_ref.shape, F32)
    m_ref[...] = jnp.full(m_ref.shape, -jnp.inf, F32)
    l_ref[...] = jnp.zeros(l_ref.shape, F32)

    def produce(kc, s_ref, c_ref, masked):
        k = k_ref[pl.ds(pl.multiple_of(kc * blk, blk), blk), :]
        s = lax.dot_general(k, qs_ref[...], (((1,), (1,)), ((), ())),
                            preferred_element_type=F32)
        if masked:
            kpos = lax.broadcasted_iota(jnp.int32, s.shape, 0)
            qpos = lax.broadcasted_iota(jnp.int32, s.shape, 1) & (blk - 1)
            s = jnp.where(kpos <= qpos, s, -jnp.inf)
        s_ref[...] = s
        c_ref[...] = jnp.max(s, axis=0, keepdims=True)

    def consume(kc, s_ref, c_ref):
        m = m_ref[...]
        m_new = jnp.maximum(m, c_ref[...])
        alpha = jnp.exp2(m - m_new)
        p = jnp.exp2(s_ref[...] - m_new)
        l_ref[...] = alpha * l_ref[...] + jnp.sum(p, axis=0, keepdims=True)
        pv = jnp.dot(vt_ref[kc], p.astype(BF16), preferred_element_type=F32)
        acc_ref[...] = acc_ref[...] * alpha + pv
        m_ref[...] = m_new

    buf_a = (sa_ref, ca_ref)
    buf_b = (sb_ref, cb_ref)

    @pl.when(qi == 0)
    def _():
        produce(0, *buf_a, True)
        consume(0, *buf_a)

    @pl.when(qi > 0)
    def _():
        produce(0, *buf_a, False)

    def pair(t, carry):
        j = 2 * t
        produce(j + 1, *buf_b, False)
        consume(j, *buf_a)
        produce(j + 2, *buf_a, False)
        consume(j + 1, *buf_b)
        return carry

    n_pairs = jnp.maximum(qi - 1, 0) // 2
    lax.fori_loop(0, n_pairs, pair, 0)
    rest = qi - 2 * n_pairs

    @pl.when(rest == 1)
    def _():
        produce(qi, *buf_b, True)
        consume(qi - 1, *buf_a)
        consume(qi, *buf_b)

    @pl.when(rest == 2)
    def _():
        produce(qi - 1, *buf_b, False)
        consume(qi - 2, *buf_a)
        produce(qi, *buf_a, True)
        consume(qi - 1, *buf_b)
        consume(qi, *buf_a)

    lq = lam_ref[...]
    lam = (jnp.exp(jnp.sum(lq[0:1] * lq[1:2], axis=-1, keepdims=True))
           - jnp.exp(jnp.sum(lq[2:3] * lq[3:4], axis=-1, keepdims=True)) + lam_init)
    o_t = acc_ref[...] / l_ref[...]
    o = jnp.transpose(o_t[:, 0:blk] - lam * o_t[:, blk:2 * blk])
    ms = jnp.mean(o * o, axis=-1, keepdims=True)
    y = (o * lax.rsqrt(ms + EPS)) * g_ref[...] * (1.0 - lam_init)
    az = az_ref[...].astype(F32)
    o_ref[...] = (y * (az * _sigmoid(az))).astype(BF16)


def _attention(qk, vt, p, lambda_qk, attn_norm_g, *, bsz, seq, blk, lam_init):
    m = bsz * seq
    nb = seq // blk

    def q_map(bh, qi):
        return (bh % HEADS, (bh // HEADS) * nb + qi, 0)

    return pl.pallas_call(
        functools.partial(_attn_kernel, blk=blk, lam_init=lam_init),
        grid=(bsz * HEADS, nb),
        in_specs=[
            pl.BlockSpec((None, blk, LANES), q_map),
            pl.BlockSpec((None, seq, LANES), lambda bh, qi: (HEADS + bh % HEADS, bh // HEADS, 0)),
            pl.BlockSpec((None, nb, V_DIM, blk), lambda bh, qi: (bh % HEADS, bh // HEADS, 0, 0)),
            pl.BlockSpec((None, blk, LANES), lambda bh, qi: (SLAB_AZ + bh % HEADS, (bh // HEADS) * nb + qi, 0)),
            pl.BlockSpec((4, QK_DIM), lambda bh, qi: (0, 0)),
            pl.BlockSpec((1, V_DIM), lambda bh, qi: (0, 0)),
        ],
        out_specs=pl.BlockSpec((None, blk, LANES), q_map),
        out_shape=jax.ShapeDtypeStruct((HEADS, m, LANES), BF16),
        scratch_shapes=[
            pltpu.VMEM((2 * blk, LANES), BF16),
            pltpu.VMEM((blk, 2 * blk), F32),
            pltpu.VMEM((blk, 2 * blk), F32),
            pltpu.VMEM((1, 2 * blk), F32),
            pltpu.VMEM((1, 2 * blk), F32),
            pltpu.VMEM((1, 2 * blk), F32),
            pltpu.VMEM((1, 2 * blk), F32),
            pltpu.VMEM((V_DIM, 2 * blk), F32),
        ],
        compiler_params=pltpu.CompilerParams(
            dimension_semantics=("arbitrary", "arbitrary"), vmem_limit_bytes=VMEM_LIMIT),
        name="diffattn",
    )(qk, qk, vt, p, lambda_qk, attn_norm_g)


def _log_sigmoid(x):
    return jnp.minimum(x, 0.0) - jnp.log(1.0 + jnp.exp(-jnp.abs(x)))


def _mlstm_kernel(qk_ref, v_ref, bo_ref, bz_ref, gate_ref, gbias_ref, cw_ref, cb_ref, ng_ref,
                  o_ref, tail_ref, c_ref, n_ref, m_ref, *, chunk):
    L = chunk
    n_pair = HEADS // 2

    @pl.when(pl.program_id(1) == 0)
    def _():
        tail_ref[...] = jnp.zeros(tail_ref.shape, F32)
        c_ref[...] = jnp.zeros(c_ref.shape, F32)
        n_ref[...] = jnp.zeros(n_ref.shape, F32)
        m_ref[...] = jnp.full(m_ref.shape, NEG_INIT, F32)

    row8 = lax.broadcasted_iota(jnp.int32, (SUBLANES, LANES), 0)
    conv = []
    for sl in range(2 * n_pair):
        x = qk_ref[sl].astype(F32)
        tail = tail_ref[sl]
        y = x * cw_ref[(CONV_WIDTH - 1) * 8 + sl: (CONV_WIDTH - 1) * 8 + sl + 1, :] + cb_ref[sl:sl + 1, :]
        for d in range(1, CONV_WIDTH):
            xr = pltpu.roll(x, d, axis=0)
            tr = pltpu.roll(tail, d, axis=0)
            first = jnp.where(row8 < d, tr, xr[0:SUBLANES])
            xs = jnp.concatenate([first, xr[SUBLANES:]], axis=0)
            j = CONV_WIDTH - 1 - d
            y = y + xs * cw_ref[j * 8 + sl: j * 8 + sl + 1, :]
        tail_ref[sl] = x[L - SUBLANES:L]
        conv.append(y * _sigmoid(y))
    lane = lax.broadcasted_iota(jnp.int32, (L, LANES), 1)
    lo_half = lane < QK_DIM

    g = gate_ref[...] + gbias_ref[...]
    ls = _log_sigmoid(g)
    r_i = lax.broadcasted_iota(jnp.int32, (L, L), 0)
    c_i = lax.broadcasted_iota(jnp.int32, (L, L), 1)
    causal = c_i <= r_i
    tri = jnp.where(causal, 1.0, 0.0).astype(BF16)
    ls_hi, ls_lo = _split_hi_lo(ls)
    bcum = (jnp.dot(tri, ls_hi, preferred_element_type=F32)
            + jnp.dot(tri, ls_lo, preferred_element_type=F32))
    g_t = jnp.transpose(g)
    b_t = jnp.transpose(bcum)

    for pr in range(n_pair):
        q_slab = conv[pr] * (QK_DIM ** -0.5)
        k_slab = conv[n_pair + pr]
        q_bf = q_slab.astype(BF16)
        zero = jnp.zeros_like(q_bf)
        q_stack = jnp.concatenate([jnp.where(lo_half, q_bf, zero), jnp.where(lo_half, zero, q_bf)], axis=0)
        k_bf = k_slab.astype(BF16)
        s_pair = lax.dot_general(q_stack, k_bf, (((1,), (1,)), ((), ())), preferred_element_type=F32)
        c_pair = c_ref[pr]
        qc_pair = jnp.dot(q_stack, c_pair.astype(BF16), preferred_element_type=F32)
        n_pair_row = n_ref[pr]
        qn = q_slab * n_pair_row

        w_cols = []
        a_vals = []
        for half in range(2):
            h = 2 * pr + half
            bcol = bcum[:, HEADS + h: HEADS + h + 1]
            icol = g[:, h: h + 1]
            brow = b_t[HEADS + h: HEADS + h + 1, :]
            irow = g_t[h: h + 1, :]
            blast = brow[:, L - 1: L]
            m_prev = m_ref[h]

            dmat = jnp.where(causal, bcol + (irow - brow), -jnp.inf)
            inter_log = bcol + m_prev
            m_t = jnp.maximum(inter_log, jnp.max(dmat, axis=-1, keepdims=True))
            sc = s_pair[half * L:(half + 1) * L] * jnp.exp(dmat - m_t)
            inter_w = jnp.exp(inter_log - m_t)
            v_h = v_ref[h]
            num = (jnp.dot(sc.astype(BF16), v_h, preferred_element_type=F32)
                   + inter_w * qc_pair[half * L:(half + 1) * L])
            qn_h = jnp.sum(jnp.where(lo_half if half == 0 else ~lo_half, qn, 0.0), axis=-1, keepdims=True)
            den = jnp.sum(sc, axis=-1, keepdims=True) + inter_w * qn_h
            hh = num / jnp.maximum(jnp.abs(den), jnp.exp(-m_t))

            ms = jnp.mean(hh * hh, axis=-1, keepdims=True)
            y = (hh * lax.rsqrt(ms + EPS)) * ng_ref[...]
            bo = bo_ref[h].astype(F32)
            bz = bz_ref[h].astype(F32)
            o_ref[h] = (_sigmoid(bo) * y * (bz * _sigmoid(bz))).astype(BF16)

            m_loc = jnp.max(blast - brow + irow, axis=-1, keepdims=True)
            m_new = jnp.maximum(blast + m_prev, m_loc)
            a_vals.append(jnp.exp(blast + m_prev - m_new))
            w_cols.append(jnp.exp((blast - m_new) - bcol + icol))
            m_ref[h] = m_new

        wk = k_slab * jnp.where(lo_half, w_cols[0], w_cols[1])
        wk_bf = wk.astype(BF16)
        zero = jnp.zeros_like(wk_bf)
        upd = (lax.dot_general(jnp.where(lo_half, wk_bf, zero), v_ref[2 * pr], (((0,), (0,)), ((), ())),
                               preferred_element_type=F32)
               + lax.dot_general(jnp.where(lo_half, zero, wk_bf), v_ref[2 * pr + 1], (((0,), (0,)), ((), ())),
                                 preferred_element_type=F32))
        row = lax.broadcasted_iota(jnp.int32, (LANES, LANES), 0)
        c_ref[pr] = jnp.where(row < QK_DIM, a_vals[0], a_vals[1]) * c_pair + upd
        lane1 = lax.broadcasted_iota(jnp.int32, (1, LANES), 1)
        n_ref[pr] = (jnp.where(lane1 < QK_DIM, a_vals[0], a_vals[1]) * n_pair_row
                     + jnp.sum(wk, axis=0, keepdims=True))


def _mlstm(p, gates, gate_bias, conv_w, conv_b, norm_g, *, bsz, seq, chunk):
    m = bsz * seq
    nc = seq // chunk

    def slab_map(group):
        return lambda b, c: (group, b * nc + c, 0)

    return pl.pallas_call(
        functools.partial(_mlstm_kernel, chunk=chunk),
        grid=(bsz, nc),
        in_specs=[
            pl.BlockSpec((HEADS, chunk, LANES), slab_map(SLAB_BQK // HEADS)),
            pl.BlockSpec((HEADS, chunk, LANES), slab_map(SLAB_BV // HEADS)),
            pl.BlockSpec((HEADS, chunk, LANES), slab_map(SLAB_BO // HEADS)),
            pl.BlockSpec((HEADS, chunk, LANES), slab_map(SLAB_BZ // HEADS)),
            pl.BlockSpec((chunk, LANES), lambda b, c: (b * nc + c, 0)),
            pl.BlockSpec((1, LANES), lambda b, c: (0, 0)),
            pl.BlockSpec((CONV_WIDTH * 8, LANES), lambda b, c: (0, 0)),
            pl.BlockSpec((8, LANES), lambda b, c: (0, 0)),
            pl.BlockSpec((1, V_DIM), lambda b, c: (0, 0)),
        ],
        out_specs=pl.BlockSpec((HEADS, chunk, LANES), lambda b, c: (0, b * nc + c, 0)),
        out_shape=jax.ShapeDtypeStruct((HEADS, m, LANES), BF16),
        scratch_shapes=[
            pltpu.VMEM((HEADS, SUBLANES, LANES), F32),
            pltpu.VMEM((HEADS // 2, LANES, V_DIM), F32),
            pltpu.VMEM((HEADS // 2, 1, LANES), F32),
            pltpu.VMEM((HEADS, 1, 1), F32),
        ],
        compiler_params=pltpu.CompilerParams(
            dimension_semantics=("arbitrary", "arbitrary"), vmem_limit_bytes=VMEM_LIMIT),
        name="mlstm",
    )(p, p, p, p, gates, gate_bias, conv_w, conv_b, norm_g)


def _outproj_kernel(oa_ref, hb_ref, ga_ref, gb_ref, x_ref, wa_ref, wb_ref, wo_ref, o_ref):
    def cat(ref):
        return jnp.concatenate([ref[h] for h in range(HEADS)], axis=1)

    ya = jnp.dot(cat(oa_ref), wa_ref[...], preferred_element_type=F32)
    yb = jnp.dot(cat(hb_ref), wb_ref[...], preferred_element_type=F32)
    u = _sigmoid(cat(ga_ref).astype(F32)) * ya + _sigmoid(cat(gb_ref).astype(F32)) * yb
    o_ref[...] = x_ref[...] + jnp.dot(u.astype(BF16), wo_ref[...], preferred_element_type=F32)


def _outproj(oa, hb, p, x2, wa, wb, wo, *, tm):
    m = x2.shape[0]
    w_spec = pl.BlockSpec((D_MODEL, D_MODEL), lambda i: (0, 0))
    return pl.pallas_call(
        _outproj_kernel,
        grid=(m // tm,),
        in_specs=[
            pl.BlockSpec((HEADS, tm, LANES), lambda i: (0, i, 0)),
            pl.BlockSpec((HEADS, tm, LANES), lambda i: (0, i, 0)),
            pl.BlockSpec((HEADS, tm, LANES), lambda i: (SLAB_GA // HEADS, i, 0)),
            pl.BlockSpec((HEADS, tm, LANES), lambda i: (SLAB_GB // HEADS, i, 0)),
            pl.BlockSpec((tm, D_MODEL), lambda i: (i, 0)),
            w_spec, w_spec, w_spec,
        ],
        out_specs=pl.BlockSpec((tm, D_MODEL), lambda i: (i, 0)),
        out_shape=jax.ShapeDtypeStruct((m, D_MODEL), F32),
        compiler_params=pltpu.CompilerParams(
            dimension_semantics=("arbitrary",), vmem_limit_bytes=VMEM_LIMIT),
        name="outproj",
    )(oa, hb, p, p, x2, wa, wb, wo)


def _reorder_w_in(w_in):
    idx = [0]
    for s in _REF_SPLITS:
        idx.append(idx[-1] + s)
    parts = [w_in[:, idx[i]:idx[i + 1]] for i in range(len(_REF_SPLITS))]
    aq, ak, av, az, bqk, bv, bi, bf, bo, bz, ga, gb = parts
    w_main = jnp.concatenate([aq, ak, av, az, bqk, bv, bo, bz, ga, gb], axis=1).astype(BF16)
    w_gate = jnp.concatenate([bi, bf, jnp.zeros((D_MODEL, LANES - 2 * HEADS), w_in.dtype)], axis=1).astype(BF16)
    return w_main, w_gate


def _pick(n, pref):
    return pref if n % pref == 0 else n


def kernel(x, positions, norm_g, w_in, q_norm_g, k_norm_g, lambda_qk, attn_norm_g, w_out_a,
           conv_w, conv_b, igate_b, fgate_b, mlstm_norm_g, w_out_b, w_o):
    bsz, seq, _ = x.shape
    m = bsz * seq
    depth = w_in.shape[0]
    tm = _pick(m, 1024)
    blk = _pick(seq, 512)
    chunk = _pick(seq, 256)

    inv_freq = ROPE_THETA ** (-jnp.arange(0, QK_DIM, 2, dtype=F32) / QK_DIM)
    ang = positions.astype(F32).reshape(m, 1) * inv_freq[None, :]
    cos, sin = jnp.cos(ang), jnp.sin(ang)
    cos4 = jnp.concatenate([cos, cos, cos, cos], axis=1)
    sin4 = jnp.concatenate([-sin, sin, -sin, sin], axis=1)
    grp = jnp.arange(LANES) // QK_DIM
    ones_blk = (grp[:, None] == grp[None, :]).astype(BF16)

    x2 = x.reshape(m, D_MODEL)
    for l in range(depth):
        lam_init = 0.8 - 0.6 * math.exp(-0.3 * l)
        w_main, w_gate = _reorder_w_in(w_in[l])
        p, gates = _inproj(x2, norm_g[l][None, :], w_main, w_gate, tm=tm, tn=1024)

        gq = jnp.tile(q_norm_g[l], 2) * (QK_DIM ** -0.5 * LOG2_E)
        gk = jnp.tile(k_norm_g[l], 2)
        gains = jnp.concatenate([jnp.broadcast_to(gq, (HEADS, LANES)),
                                 jnp.broadcast_to(gk, (HEADS, LANES))], axis=0)[:, None, :]
        qk = _qkprep(p, cos4, sin4, gains, ones_blk, tm=tm)
        vt = _vprep(p, tm=tm, blk=blk)
        oa = _attention(qk, vt, p, lambda_qk[l], attn_norm_g[l][None, :],
                        bsz=bsz, seq=seq, blk=blk, lam_init=lam_init)

        gate_bias = jnp.concatenate([igate_b[l], fgate_b[l], jnp.zeros((LANES - 2 * HEADS,), F32)])[None, :]
        cw = conv_w[l].reshape(CONV_WIDTH * 8, LANES)
        cb = conv_b[l].reshape(8, LANES)
        hb = _mlstm(p, gates, gate_bias, cw, cb, mlstm_norm_g[l][None, :], bsz=bsz, seq=seq, chunk=chunk)

        x2 = _outproj(oa, hb, p, x2, w_out_a[l].astype(BF16), w_out_b[l].astype(BF16),
                      w_o[l].astype(BF16), tm=_pick(m, 512))
    return x2.reshape(bsz, seq, D_MODEL)
```

```python
import functools
import math

import jax
import jax.numpy as jnp
from jax import lax
from jax.experimental import pallas as pl
from jax.experimental.pallas import tpu as pltpu

F32 = jnp.float32
BF16 = jnp.bfloat16

D_MODEL = 1024
HEADS = 8
QK_DIM = 64
V_DIM = 128
CONV_WIDTH = 4
ROPE_THETA = 10000.0
EPS = 1e-6
NEG_INIT = -1e30
LOG2_E = math.log2(math.e)
LANES = 128
SUBLANES = 8
VMEM_LIMIT = 48 * 1024 * 1024

SLAB_AQ, SLAB_AK, SLAB_AV, SLAB_AZ = 0, 8, 16, 24
SLAB_BQK, SLAB_BV, SLAB_BO, SLAB_BZ, SLAB_GA, SLAB_GB = 32, 40, 48, 56, 64, 72
N_SLABS = 80
N_MAIN = N_SLABS * LANES

_REF_SPLITS = (1024, 1024, 1024, 1024, 1024, 1024, 8, 8, 1024, 1024, 1024, 1024)


def _split_hi_lo(x):
    hi = x.astype(BF16)
    lo = (x - hi.astype(F32)).astype(BF16)
    return hi, lo


def _sigmoid(x):
    return 1.0 / (1.0 + jnp.exp(-x))


def _inproj_kernel(x_ref, g_ref, w_ref, wg_ref, p_ref, gate_ref, h_ref, *, n_sub):
    @pl.when(pl.program_id(1) == 0)
    def _():
        x = x_ref[...]
        ms = jnp.mean(x * x, axis=-1, keepdims=True)
        h = ((x * lax.rsqrt(ms + EPS)) * g_ref[...]).astype(BF16)
        h_ref[...] = h
        gate_ref[...] = jnp.dot(h, wg_ref[...], preferred_element_type=F32)

    acc = jnp.dot(h_ref[...], w_ref[...], preferred_element_type=F32)
    for c in range(n_sub):
        p_ref[c] = acc[:, c * LANES:(c + 1) * LANES].astype(BF16)


def _inproj(x2, norm_g, w_main, w_gate, *, tm, tn):
    m = x2.shape[0]
    n_sub = tn // LANES
    return pl.pallas_call(
        functools.partial(_inproj_kernel, n_sub=n_sub),
        grid=(m // tm, N_MAIN // tn),
        in_specs=[
            pl.BlockSpec((tm, D_MODEL), lambda i, j: (i, 0)),
            pl.BlockSpec((1, D_MODEL), lambda i, j: (0, 0)),
            pl.BlockSpec((D_MODEL, tn), lambda i, j: (0, j)),
            pl.BlockSpec((D_MODEL, LANES), lambda i, j: (0, 0)),
        ],
        out_specs=[
            pl.BlockSpec((n_sub, tm, LANES), lambda i, j: (j, i, 0)),
            pl.BlockSpec((tm, LANES), lambda i, j: (i, 0)),
        ],
        out_shape=[
            jax.ShapeDtypeStruct((N_SLABS, m, LANES), BF16),
            jax.ShapeDtypeStruct((m, LANES), F32),
        ],
        scratch_shapes=[pltpu.VMEM((tm, D_MODEL), BF16)],
        compiler_params=pltpu.CompilerParams(
            dimension_semantics=("arbitrary", "arbitrary"), vmem_limit_bytes=VMEM_LIMIT),
        name="inproj",
    )(x2, norm_g, w_main, w_gate)


def _attnprep_kernel(q_ref, k_ref, v_ref, cos_ref, sin_ref, g_ref, ones_ref, qo_ref, ko_ref, vt_ref,
                     *, blk, n_sub):
    x = jnp.concatenate([q_ref[...], k_ref[...]], axis=1).astype(F32)
    hi, lo = _split_hi_lo(x * x)
    ones = ones_ref[...]
    ss = (jnp.dot(hi, ones, preferred_element_type=F32)
          + jnp.dot(lo, ones, preferred_element_type=F32))
    y = (x * lax.rsqrt(ss * (1.0 / QK_DIM) + EPS)) * g_ref[...]
    cos = cos_ref[...]
    sin = sin_ref[...]
    lane = lax.broadcasted_iota(jnp.int32, cos.shape, 1)
    half = QK_DIM // 2
    for side, o_ref in enumerate((qo_ref, ko_ref)):
        ys = y[:, side * LANES:(side + 1) * LANES]
        partner = jnp.where((lane & half) == 0,
                            pltpu.roll(ys, LANES - half, axis=1),
                            pltpu.roll(ys, half, axis=1))
        o_ref[...] = (ys * cos + partner * sin).astype(BF16)
    for c in range(n_sub):
        vt_ref[c] = jnp.transpose(v_ref[c * blk:(c + 1) * blk, :].astype(F32)).astype(BF16)


def _attnprep(p, cos4, sin4, gains, ones_blk, *, tm, blk):
    m = p.shape[1]
    n_sub = tm // blk
    return pl.pallas_call(
        functools.partial(_attnprep_kernel, blk=blk, n_sub=n_sub),
        grid=(m // tm, HEADS),
        in_specs=[
            pl.BlockSpec((None, tm, LANES), lambda i, h: (SLAB_AQ + h, i, 0)),
            pl.BlockSpec((None, tm, LANES), lambda i, h: (SLAB_AK + h, i, 0)),
            pl.BlockSpec((None, tm, LANES), lambda i, h: (SLAB_AV + h, i, 0)),
            pl.BlockSpec((tm, LANES), lambda i, h: (i, 0)),
            pl.BlockSpec((tm, LANES), lambda i, h: (i, 0)),
            pl.BlockSpec((1, 2 * LANES), lambda i, h: (0, 0)),
            pl.BlockSpec((2 * LANES, 2 * LANES), lambda i, h: (0, 0)),
        ],
        out_specs=[
            pl.BlockSpec((None, tm, LANES), lambda i, h: (h, i, 0)),
            pl.BlockSpec((None, tm, LANES), lambda i, h: (h, i, 0)),
            pl.BlockSpec((None, n_sub, V_DIM, blk), lambda i, h: (h, i, 0, 0)),
        ],
        out_shape=[
            jax.ShapeDtypeStruct((HEADS, m, LANES), BF16),
            jax.ShapeDtypeStruct((HEADS, m, LANES), BF16),
            jax.ShapeDtypeStruct((HEADS, m // blk, V_DIM, blk), BF16),
        ],
        compiler_params=pltpu.CompilerParams(
            dimension_semantics=("arbitrary", "arbitrary"), vmem_limit_bytes=VMEM_LIMIT),
        name="attnprep",
    )(p, p, p, cos4, sin4, gains, ones_blk)


def _attn_kernel(q_ref, k_ref, vt_ref, az_ref, lam_ref, g_ref, o_ref,
                 qs_ref, sa_ref, sb_ref, ca_ref, cb_ref, m_ref, l_ref, acc_ref, *, blk, lam_init):
    nb = q_ref.shape[0] // blk

    def produce(kc, s_ref, c_ref, masked):
        k = k_ref[pl.ds(pl.multiple_of(kc * blk, blk), blk), :]
        s = lax.dot_general(k, qs_ref[...], (((1,), (1,)), ((), ())),
                            preferred_element_type=F32)
        if masked:
            kpos = lax.broadcasted_iota(jnp.int32, s.shape, 0)
            qpos = lax.broadcasted_iota(jnp.int32, s.shape, 1) & (blk - 1)
            s = jnp.where(kpos <= qpos, s, -jnp.inf)
        s_ref[...] = s
        c_ref[...] = jnp.max(s, axis=0, keepdims=True)

    def consume(kc, s_ref, c_ref):
        m = m_ref[...]
        m_new = jnp.maximum(m, c_ref[...])
        alpha = jnp.exp2(m - m_new)
        p = jnp.exp2(s_ref[...] - m_new)
        l_ref[...] = alpha * l_ref[...] + jnp.sum(p, axis=0, keepdims=True)
        pv = jnp.dot(vt_ref[kc], p.astype(BF16), preferred_element_type=F32)
        acc_ref[...] = acc_ref[...] * alpha + pv
        m_ref[...] = m_new

    buf_a = (sa_ref, ca_ref)
    buf_b = (sb_ref, cb_ref)

    def pair(t, carry):
        j = 2 * t
        produce(j + 1, *buf_b, False)
        consume(j, *buf_a)
        produce(j + 2, *buf_a, False)
        consume(j + 1, *buf_b)
        return carry

    lq = lam_ref[...]
    lam = (jnp.exp(jnp.sum(lq[0:1] * lq[1:2], axis=-1, keepdims=True))
           - jnp.exp(jnp.sum(lq[2:3] * lq[3:4], axis=-1, keepdims=True)) + lam_init)

    def qblock(qi, carry):
        rows = pl.ds(pl.multiple_of(qi * blk, blk), blk)
        q = q_ref[rows, :]
        lane = lax.broadcasted_iota(jnp.int32, q.shape, 1)
        zero = jnp.zeros_like(q)
        qs_ref[0:blk, :] = jnp.where(lane < QK_DIM, q, zero)
        qs_ref[blk:2 * blk, :] = jnp.where(lane >= QK_DIM, q, zero)
        acc_ref[...] = jnp.zeros(acc_ref.shape, F32)
        m_ref[...] = jnp.full(m_ref.shape, -jnp.inf, F32)
        l_ref[...] = jnp.zeros(l_ref.shape, F32)

        @pl.when(qi == 0)
        def _():
            produce(0, *buf_a, True)
            consume(0, *buf_a)

        @pl.when(qi > 0)
        def _():
            produce(0, *buf_a, False)

        n_pairs = jnp.maximum(qi - 1, 0) // 2
        lax.fori_loop(0, n_pairs, pair, 0)
        rest = qi - 2 * n_pairs

        @pl.when(rest == 1)
        def _():
            produce(qi, *buf_b, True)
            consume(qi - 1, *buf_a)
            consume(qi, *buf_b)

        @pl.when(rest == 2)
        def _():
            produce(qi - 1, *buf_b, False)
            consume(qi - 2, *buf_a)
            produce(qi, *buf_a, True)
            consume(qi - 1, *buf_b)
            consume(qi, *buf_a)

        o_t = acc_ref[...] / l_ref[...]
        o = jnp.transpose(o_t[:, 0:blk] - lam * o_t[:, blk:2 * blk])
        ms = jnp.mean(o * o, axis=-1, keepdims=True)
        y = (o * lax.rsqrt(ms + EPS)) * g_ref[...] * (1.0 - lam_init)
        az = az_ref[rows, :].astype(F32)
        o_ref[rows, :] = (y * (az * _sigmoid(az))).astype(BF16)
        return carry

    lax.fori_loop(0, nb, qblock, 0)


def _attention(q, k, vt, p, lambda_qk, attn_norm_g, *, bsz, seq, blk, lam_init):
    m = bsz * seq
    nb = seq // blk

    return pl.pallas_call(
        functools.partial(_attn_kernel, blk=blk, lam_init=lam_init),
        grid=(bsz * HEADS,),
        in_specs=[
            pl.BlockSpec((None, seq, LANES), lambda bh: (bh % HEADS, bh // HEADS, 0)),
            pl.BlockSpec((None, seq, LANES), lambda bh: (bh % HEADS, bh // HEADS, 0)),
            pl.BlockSpec((None, nb, V_DIM, blk), lambda bh: (bh % HEADS, bh // HEADS, 0, 0)),
            pl.BlockSpec((None, seq, LANES), lambda bh: (SLAB_AZ + bh % HEADS, bh // HEADS, 0)),
            pl.BlockSpec((4, QK_DIM), lambda bh: (0, 0)),
            pl.BlockSpec((1, V_DIM), lambda bh: (0, 0)),
        ],
        out_specs=pl.BlockSpec((None, seq, LANES), lambda bh: (bh % HEADS, bh // HEADS, 0)),
        out_shape=jax.ShapeDtypeStruct((HEADS, m, LANES), BF16),
        scratch_shapes=[
            pltpu.VMEM((2 * blk, LANES), BF16),
            pltpu.VMEM((blk, 2 * blk), F32),
            pltpu.VMEM((blk, 2 * blk), F32),
            pltpu.VMEM((1, 2 * blk), F32),
            pltpu.VMEM((1, 2 * blk), F32),
            pltpu.VMEM((1, 2 * blk), F32),
            pltpu.VMEM((1, 2 * blk), F32),
            pltpu.VMEM((V_DIM, 2 * blk), F32),
        ],
        compiler_params=pltpu.CompilerParams(
            dimension_semantics=("arbitrary",), vmem_limit_bytes=VMEM_LIMIT),
        name="diffattn",
    )(q, k, vt, p, lambda_qk, attn_norm_g)


def _log_sigmoid(x):
    return jnp.minimum(x, 0.0) - jnp.log(1.0 + jnp.exp(-jnp.abs(x)))


def _mlstm_kernel(qk_ref, v_ref, bo_ref, bz_ref, gate_ref, gbias_ref, cw_ref, cb_ref, ng_ref,
                  o_ref, tail_ref, c_ref, n_ref, m_ref, *, chunk):
    L = chunk
    n_pair = HEADS // 2

    @pl.when(pl.program_id(1) == 0)
    def _():
        tail_ref[...] = jnp.zeros(tail_ref.shape, F32)
        c_ref[...] = jnp.zeros(c_ref.shape, F32)
        n_ref[...] = jnp.zeros(n_ref.shape, F32)
        m_ref[...] = jnp.full(m_ref.shape, NEG_INIT, F32)

    row8 = lax.broadcasted_iota(jnp.int32, (SUBLANES, LANES), 0)
    conv = []
    for sl in range(2 * n_pair):
        x = qk_ref[sl].astype(F32)
        tail = tail_ref[sl]
        y = x * cw_ref[(CONV_WIDTH - 1) * 8 + sl: (CONV_WIDTH - 1) * 8 + sl + 1, :] + cb_ref[sl:sl + 1, :]
        for d in range(1, CONV_WIDTH):
            xr = pltpu.roll(x, d, axis=0)
            tr = pltpu.roll(tail, d, axis=0)
            first = jnp.where(row8 < d, tr, xr[0:SUBLANES])
            xs = jnp.concatenate([first, xr[SUBLANES:]], axis=0)
            j = CONV_WIDTH - 1 - d
            y = y + xs * cw_ref[j * 8 + sl: j * 8 + sl + 1, :]
        tail_ref[sl] = x[L - SUBLANES:L]
        conv.append(y * _sigmoid(y))
    lane = lax.broadcasted_iota(jnp.int32, (L, LANES), 1)
    lo_half = lane < QK_DIM

    g = gate_ref[...] + gbias_ref[...]
    ls = _log_sigmoid(g)
    r_i = lax.broadcasted_iota(jnp.int32, (L, L), 0)
    c_i = lax.broadcasted_iota(jnp.int32, (L, L), 1)
    causal = c_i <= r_i
    tri = jnp.where(causal, 1.0, 0.0).astype(BF16)
    ls_hi, ls_lo = _split_hi_lo(ls)
    bcum = (jnp.dot(tri, ls_hi, preferred_element_type=F32)
            + jnp.dot(tri, ls_lo, preferred_element_type=F32))
    g_t = jnp.transpose(g)
    b_t = jnp.transpose(bcum)

    for pr in range(n_pair):
        q_slab = conv[pr] * (QK_DIM ** -0.5)
        k_slab = conv[n_pair + pr]
        q_bf = q_slab.astype(BF16)
        zero = jnp.zeros_like(q_bf)
        q_stack = jnp.concatenate([jnp.where(lo_half, q_bf, zero), jnp.where(lo_half, zero, q_bf)], axis=0)
        k_bf = k_slab.astype(BF16)
        s_pair = lax.dot_general(q_stack, k_bf, (((1,), (1,)), ((), ())), preferred_element_type=F32)
        c_pair = c_ref[pr]
        qc_pair = jnp.dot(q_stack, c_pair.astype(BF16), preferred_element_type=F32)
        n_pair_row = n_ref[pr]
        qn = q_slab * n_pair_row

        w_cols = []
        a_vals = []
        for half in range(2):
            h = 2 * pr + half
            bcol = bcum[:, HEADS + h: HEADS + h + 1]
            icol = g[:, h: h + 1]
            brow = b_t[HEADS + h: HEADS + h + 1, :]
            irow = g_t[h: h + 1, :]
            blast = brow[:, L - 1: L]
            m_prev = m_ref[h]

            dmat = jnp.where(causal, bcol + (irow - brow), -jnp.inf)
            inter_log = bcol + m_prev
            m_t = jnp.maximum(inter_log, jnp.max(dmat, axis=-1, keepdims=True))
            sc = s_pair[half * L:(half + 1) * L] * jnp.exp(dmat - m_t)
            inter_w = jnp.exp(inter_log - m_t)
            v_h = v_ref[h]
            num = (jnp.dot(sc.astype(BF16), v_h, preferred_element_type=F32)
                   + inter_w * qc_pair[half * L:(half + 1) * L])
            qn_h = jnp.sum(jnp.where(lo_half if half == 0 else ~lo_half, qn, 0.0), axis=-1, keepdims=True)
            den = jnp.sum(sc, axis=-1, keepdims=True) + inter_w * qn_h
            hh = num / jnp.maximum(jnp.abs(den), jnp.exp(-m_t))

            ms = jnp.mean(hh * hh, axis=-1, keepdims=True)
            y = (hh * lax.rsqrt(ms + EPS)) * ng_ref[...]
            bo = bo_ref[h].astype(F32)
            bz = bz_ref[h].astype(F32)
            o_ref[h] = (_sigmoid(bo) * y * (bz * _sigmoid(bz))).astype(BF16)

            m_loc = jnp.max(blast - brow + irow, axis=-1, keepdims=True)
            m_new = jnp.maximum(blast + m_prev, m_loc)
            a_vals.append(jnp.exp(blast + m_prev - m_new))
            w_cols.append(jnp.exp((blast - m_new) - bcol + icol))
            m_ref[h] = m_new

        wk = k_slab * jnp.where(lo_half, w_cols[0], w_cols[1])
        wk_bf = wk.astype(BF16)
        zero = jnp.zeros_like(wk_bf)
        upd = (lax.dot_general(jnp.where(lo_half, wk_bf, zero), v_ref[2 * pr], (((0,), (0,)), ((), ())),
                               preferred_element_type=F32)
               + lax.dot_general(jnp.where(lo_half, zero, wk_bf), v_ref[2 * pr + 1], (((0,), (0,)), ((), ())),
                                 preferred_element_type=F32))
        row = lax.broadcasted_iota(jnp.int32, (LANES, LANES), 0)
        c_ref[pr] = jnp.where(row < QK_DIM, a_vals[0], a_vals[1]) * c_pair + upd
        lane1 = lax.broadcasted_iota(jnp.int32, (1, LANES), 1)
        n_ref[pr] = (jnp.where(lane1 < QK_DIM, a_vals[0], a_vals[1]) * n_pair_row
                     + jnp.sum(wk, axis=0, keepdims=True))


def _mlstm(p, gates, gate_bias, conv_w, conv_b, norm_g, *, bsz, seq, chunk):
    m = bsz * seq
    nc = seq // chunk

    def slab_map(group):
        return lambda b, c: (group, b * nc + c, 0)

    return pl.pallas_call(
        functools.partial(_mlstm_kernel, chunk=chunk),
        grid=(bsz, nc),
        in_specs=[
            pl.BlockSpec((HEADS, chunk, LANES), slab_map(SLAB_BQK // HEADS)),
            pl.BlockSpec((HEADS, chunk, LANES), slab_map(SLAB_BV // HEADS)),
            pl.BlockSpec((HEADS, chunk, LANES), slab_map(SLAB_BO // HEADS)),
            pl.BlockSpec((HEADS, chunk, LANES), slab_map(SLAB_BZ // HEADS)),
            pl.BlockSpec((chunk, LANES), lambda b, c: (b * nc + c, 0)),
            pl.BlockSpec((1, LANES), lambda b, c: (0, 0)),
            pl.BlockSpec((CONV_WIDTH * 8, LANES), lambda b, c: (0, 0)),
            pl.BlockSpec((8, LANES), lambda b, c: (0, 0)),
            pl.BlockSpec((1, V_DIM), lambda b, c: (0, 0)),
        ],
        out_specs=pl.BlockSpec((HEADS, chunk, LANES), lambda b, c: (0, b * nc + c, 0)),
        out_shape=jax.ShapeDtypeStruct((HEADS, m, LANES), BF16),
        scratch_shapes=[
            pltpu.VMEM((HEADS, SUBLANES, LANES), F32),
            pltpu.VMEM((HEADS // 2, LANES, V_DIM), F32),
            pltpu.VMEM((HEADS // 2, 1, LANES), F32),
            pltpu.VMEM((HEADS, 1, 1), F32),
        ],
        compiler_params=pltpu.CompilerParams(
            dimension_semantics=("arbitrary", "arbitrary"), vmem_limit_bytes=VMEM_LIMIT),
        name="mlstm",
    )(p, p, p, p, gates, gate_bias, conv_w, conv_b, norm_g)


def _outproj_kernel(oa_ref, hb_ref, ga_ref, gb_ref, x_ref, wa_ref, wb_ref, wo_ref, o_ref):
    def cat(ref):
        return jnp.concatenate([ref[h] for h in range(HEADS)], axis=1)

    ya = jnp.dot(cat(oa_ref), wa_ref[...], preferred_element_type=F32)
    yb = jnp.dot(cat(hb_ref), wb_ref[...], preferred_element_type=F32)
    u = _sigmoid(cat(ga_ref).astype(F32)) * ya + _sigmoid(cat(gb_ref).astype(F32)) * yb
    o_ref[...] = x_ref[...] + jnp.dot(u.astype(BF16), wo_ref[...], preferred_element_type=F32)


def _outproj(oa, hb, p, x2, wa, wb, wo, *, tm):
    m = x2.shape[0]
    w_spec = pl.BlockSpec((D_MODEL, D_MODEL), lambda i: (0, 0))
    return pl.pallas_call(
        _outproj_kernel,
        grid=(m // tm,),
        in_specs=[
            pl.BlockSpec((HEADS, tm, LANES), lambda i: (0, i, 0)),
            pl.BlockSpec((HEADS, tm, LANES), lambda i: (0, i, 0)),
            pl.BlockSpec((HEADS, tm, LANES), lambda i: (SLAB_GA // HEADS, i, 0)),
            pl.BlockSpec((HEADS, tm, LANES), lambda i: (SLAB_GB // HEADS, i, 0)),
            pl.BlockSpec((tm, D_MODEL), lambda i: (i, 0)),
            w_spec, w_spec, w_spec,
        ],
        out_specs=pl.BlockSpec((tm, D_MODEL), lambda i: (i, 0)),
        out_shape=jax.ShapeDtypeStruct((m, D_MODEL), F32),
        compiler_params=pltpu.CompilerParams(
            dimension_semantics=("arbitrary",), vmem_limit_bytes=VMEM_LIMIT),
        name="outproj",
    )(oa, hb, p, p, x2, wa, wb, wo)


def _reorder_w_in(w_in):
    idx = [0]
    for s in _REF_SPLITS:
        idx.append(idx[-1] + s)
    parts = [w_in[:, idx[i]:idx[i + 1]] for i in range(len(_REF_SPLITS))]
    aq, ak, av, az, bqk, bv, bi, bf, bo, bz, ga, gb = parts
    w_main = jnp.concatenate([aq, ak, av, az, bqk, bv, bo, bz, ga, gb], axis=1).astype(BF16)
    w_gate = jnp.concatenate([bi, bf, jnp.zeros((D_MODEL, LANES - 2 * HEADS), w_in.dtype)], axis=1).astype(BF16)
    return w_main, w_gate


def _pick(n, pref):
    return pref if n % pref == 0 else n


def kernel(x, positions, norm_g, w_in, q_norm_g, k_norm_g, lambda_qk, attn_norm_g, w_out_a,
           conv_w, conv_b, igate_b, fgate_b, mlstm_norm_g, w_out_b, w_o):
    bsz, seq, _ = x.shape
    m = bsz * seq
    depth = w_in.shape[0]
    tm = _pick(m, 1024)
    blk = _pick(seq, 512)
    chunk = _pick(seq, 256)

    inv_freq = ROPE_THETA ** (-jnp.arange(0, QK_DIM, 2, dtype=F32) / QK_DIM)
    ang = positions.astype(F32).reshape(m, 1) * inv_freq[None, :]
    cos, sin = jnp.cos(ang), jnp.sin(ang)
    cos4 = jnp.concatenate([cos, cos, cos, cos], axis=1)
    sin4 = jnp.concatenate([-sin, sin, -sin, sin], axis=1)
    grp = jnp.arange(2 * LANES) // QK_DIM
    ones_blk = (grp[:, None] == grp[None, :]).astype(BF16)

    x2 = x.reshape(m, D_MODEL)
    for l in range(depth):
        lam_init = 0.8 - 0.6 * math.exp(-0.3 * l)
        w_main, w_gate = _reorder_w_in(w_in[l])
        p, gates = _inproj(x2, norm_g[l][None, :], w_main, w_gate, tm=tm, tn=1024)

        gq = jnp.tile(q_norm_g[l], 2) * (QK_DIM ** -0.5 * LOG2_E)
        gk = jnp.tile(k_norm_g[l], 2)
        gains = jnp.concatenate([gq, gk])[None, :]
        q, k, vt = _attnprep(p, cos4, sin4, gains, ones_blk, tm=tm, blk=blk)
        oa = _attention(q, k, vt, p, lambda_qk[l], attn_norm_g[l][None, :],
                        bsz=bsz, seq=seq, blk=blk, lam_init=lam_init)

        gate_bias = jnp.concatenate([igate_b[l], fgate_b[l], jnp.zeros((LANES - 2 * HEADS,), F32)])[None, :]
        cw = conv_w[l].reshape(CONV_WIDTH * 8, LANES)
        cb = conv_b[l].reshape(8, LANES)
        hb = _mlstm(p, gates, gate_bias, cw, cb, mlstm_norm_g[l][None, :], bsz=bsz, seq=seq, chunk=chunk)

        x2 = _outproj(oa, hb, p, x2, w_out_a[l].astype(BF16), w_out_b[l].astype(BF16),
                      w_o[l].astype(BF16), tm=_pick(m, 512))
    return x2.reshape(bsz, seq, D_MODEL)
```

```python
import functools
import math

import jax
import jax.numpy as jnp
from jax import lax
from jax.experimental import pallas as pl
from jax.experimental.pallas import tpu as pltpu

F32 = jnp.float32
BF16 = jnp.bfloat16

D_MODEL = 1024
HEADS = 8
QK_DIM = 64
V_DIM = 128
CONV_WIDTH = 4
ROPE_THETA = 10000.0
EPS = 1e-6
NEG_INIT = -1e30
LOG2_E = math.log2(math.e)
LANES = 128
SUBLANES = 8
VMEM_LIMIT = 48 * 1024 * 1024

SLAB_AQ, SLAB_AK, SLAB_AV, SLAB_AZ = 0, 8, 16, 24
SLAB_BQK, SLAB_BV, SLAB_BO, SLAB_BZ, SLAB_GA, SLAB_GB = 32, 40, 48, 56, 64, 72
N_SLABS = 80
N_MAIN = N_SLABS * LANES

_REF_SPLITS = (1024, 1024, 1024, 1024, 1024, 1024, 8, 8, 1024, 1024, 1024, 1024)


def _split_hi_lo(x):
    hi = x.astype(BF16)
    lo = (x - hi.astype(F32)).astype(BF16)
    return hi, lo


def _sigmoid(x):
    return 1.0 / (1.0 + jnp.exp2(x * (-LOG2_E)))


def _inproj_kernel(x_ref, g_ref, w_ref, wg_ref, p_ref, gate_ref, h_ref, *, n_sub):
    @pl.when(pl.program_id(1) == 0)
    def _():
        x = x_ref[...]
        ms = jnp.mean(x * x, axis=-1, keepdims=True)
        h = ((x * lax.rsqrt(ms + EPS)) * g_ref[...]).astype(BF16)
        h_ref[...] = h
        gate_ref[...] = jnp.dot(h, wg_ref[...], preferred_element_type=F32)

    acc = jnp.dot(h_ref[...], w_ref[...], preferred_element_type=F32)
    for c in range(n_sub):
        p_ref[c] = acc[:, c * LANES:(c + 1) * LANES].astype(BF16)


def _inproj(x2, norm_g, w_main, w_gate, *, tm, tn):
    m = x2.shape[0]
    n_sub = tn // LANES
    return pl.pallas_call(
        functools.partial(_inproj_kernel, n_sub=n_sub),
        grid=(m // tm, N_MAIN // tn),
        in_specs=[
            pl.BlockSpec((tm, D_MODEL), lambda i, j: (i, 0)),
            pl.BlockSpec((1, D_MODEL), lambda i, j: (0, 0)),
            pl.BlockSpec((D_MODEL, tn), lambda i, j: (0, j)),
            pl.BlockSpec((D_MODEL, LANES), lambda i, j: (0, 0)),
        ],
        out_specs=[
            pl.BlockSpec((n_sub, tm, LANES), lambda i, j: (j, i, 0)),
            pl.BlockSpec((tm, LANES), lambda i, j: (i, 0)),
        ],
        out_shape=[
            jax.ShapeDtypeStruct((N_SLABS, m, LANES), BF16),
            jax.ShapeDtypeStruct((m, LANES), F32),
        ],
        scratch_shapes=[pltpu.VMEM((tm, D_MODEL), BF16)],
        compiler_params=pltpu.CompilerParams(
            dimension_semantics=("arbitrary", "arbitrary"), vmem_limit_bytes=VMEM_LIMIT),
        name="inproj",
    )(x2, norm_g, w_main, w_gate)


def _attnprep_kernel(q_ref, k_ref, v_ref, cos_ref, sin_ref, g_ref, ones_ref, qt_ref, ko_ref, vt_ref,
                     *, blk, n_sub):
    x = jnp.concatenate([q_ref[...], k_ref[...]], axis=1).astype(F32)
    hi, lo = _split_hi_lo(x * x)
    ones = ones_ref[...]
    ss = (jnp.dot(hi, ones, preferred_element_type=F32)
          + jnp.dot(lo, ones, preferred_element_type=F32))
    y = (x * lax.rsqrt(ss * (1.0 / QK_DIM) + EPS)) * g_ref[...]
    cos = cos_ref[...]
    sin = sin_ref[...]
    lane = lax.broadcasted_iota(jnp.int32, cos.shape, 1)
    half = QK_DIM // 2
    roped = []
    for side in range(2):
        ys = y[:, side * LANES:(side + 1) * LANES]
        partner = jnp.where((lane & half) == 0,
                            pltpu.roll(ys, LANES - half, axis=1),
                            pltpu.roll(ys, half, axis=1))
        roped.append(ys * cos + partner * sin)
    ko_ref[...] = roped[1].astype(BF16)
    for c in range(n_sub):
        rows = slice(c * blk, (c + 1) * blk)
        qt_ref[c] = jnp.transpose(roped[0][rows]).astype(BF16)
        vt_ref[c] = jnp.transpose(v_ref[rows, :].astype(F32)).astype(BF16)


def _attnprep(p, cos4, sin4, gains, ones_blk, *, tm, blk):
    m = p.shape[1]
    n_sub = tm // blk
    return pl.pallas_call(
        functools.partial(_attnprep_kernel, blk=blk, n_sub=n_sub),
        grid=(m // tm, HEADS),
        in_specs=[
            pl.BlockSpec((None, tm, LANES), lambda i, h: (SLAB_AQ + h, i, 0)),
            pl.BlockSpec((None, tm, LANES), lambda i, h: (SLAB_AK + h, i, 0)),
            pl.BlockSpec((None, tm, LANES), lambda i, h: (SLAB_AV + h, i, 0)),
            pl.BlockSpec((tm, LANES), lambda i, h: (i, 0)),
            pl.BlockSpec((tm, LANES), lambda i, h: (i, 0)),
            pl.BlockSpec((1, 2 * LANES), lambda i, h: (0, 0)),
            pl.BlockSpec((2 * LANES, 2 * LANES), lambda i, h: (0, 0)),
        ],
        out_specs=[
            pl.BlockSpec((None, n_sub, LANES, blk), lambda i, h: (h, i, 0, 0)),
            pl.BlockSpec((None, tm, LANES), lambda i, h: (h, i, 0)),
            pl.BlockSpec((None, n_sub, V_DIM, blk), lambda i, h: (h, i, 0, 0)),
        ],
        out_shape=[
            jax.ShapeDtypeStruct((HEADS, m // blk, LANES, blk), BF16),
            jax.ShapeDtypeStruct((HEADS, m, LANES), BF16),
            jax.ShapeDtypeStruct((HEADS, m // blk, V_DIM, blk), BF16),
        ],
        compiler_params=pltpu.CompilerParams(
            dimension_semantics=("arbitrary", "arbitrary"), vmem_limit_bytes=VMEM_LIMIT),
        name="attnprep",
    )(p, p, p, cos4, sin4, gains, ones_blk)


def _attn_kernel(q_ref, k_ref, vt_ref, az_ref, lam_ref, g_ref, o_ref,
                 qs_ref, sa_ref, sb_ref, ca_ref, cb_ref, m_ref, l_ref, acc_ref, *, blk, lam_init):
    nb = q_ref.shape[0]

    def produce(kc, s_ref, c_ref, masked):
        k = k_ref[pl.ds(pl.multiple_of(kc * blk, blk), blk), :]
        s = jnp.dot(k, qs_ref[...], preferred_element_type=F32)
        if masked:
            kpos = lax.broadcasted_iota(jnp.int32, s.shape, 0)
            qpos = lax.broadcasted_iota(jnp.int32, s.shape, 1) & (blk - 1)
            s = jnp.where(kpos <= qpos, s, -jnp.inf)
        s_ref[...] = s
        c_ref[...] = jnp.max(s, axis=0, keepdims=True)

    def consume(kc, s_ref, c_ref):
        m = m_ref[...]
        m_new = jnp.maximum(m, c_ref[...])
        alpha = jnp.exp2(m - m_new)
        p = jnp.exp2(s_ref[...] - m_new)
        l_ref[...] = alpha * l_ref[...] + jnp.sum(p, axis=0, keepdims=True)
        pv = jnp.dot(vt_ref[kc], p.astype(BF16), preferred_element_type=F32)
        acc_ref[...] = acc_ref[...] * alpha + pv
        m_ref[...] = m_new

    buf_a = (sa_ref, ca_ref)
    buf_b = (sb_ref, cb_ref)

    def pair(t, carry):
        j = 2 * t
        produce(j + 1, *buf_b, False)
        consume(j, *buf_a)
        produce(j + 2, *buf_a, False)
        consume(j + 1, *buf_b)
        return carry

    lq = lam_ref[...]
    lam = (jnp.exp(jnp.sum(lq[0:1] * lq[1:2], axis=-1, keepdims=True))
           - jnp.exp(jnp.sum(lq[2:3] * lq[3:4], axis=-1, keepdims=True)) + lam_init)

    def qblock(qi, carry):
        rows = pl.ds(pl.multiple_of(qi * blk, blk), blk)
        q_t = q_ref[qi]
        dim = lax.broadcasted_iota(jnp.int32, q_t.shape, 0)
        zero = jnp.zeros_like(q_t)
        qs_ref[:, 0:blk] = jnp.where(dim < QK_DIM, q_t, zero)
        qs_ref[:, blk:2 * blk] = jnp.where(dim >= QK_DIM, q_t, zero)
        acc_ref[...] = jnp.zeros(acc_ref.shape, F32)
        m_ref[...] = jnp.full(m_ref.shape, -jnp.inf, F32)
        l_ref[...] = jnp.zeros(l_ref.shape, F32)

        @pl.when(qi == 0)
        def _():
            produce(0, *buf_a, True)
            consume(0, *buf_a)

        @pl.when(qi > 0)
        def _():
            produce(0, *buf_a, False)

        n_pairs = jnp.maximum(qi - 1, 0) // 2
        lax.fori_loop(0, n_pairs, pair, 0)
        rest = qi - 2 * n_pairs

        @pl.when(rest == 1)
        def _():
            produce(qi, *buf_b, True)
            consume(qi - 1, *buf_a)
            consume(qi, *buf_b)

        @pl.when(rest == 2)
        def _():
            produce(qi - 1, *buf_b, False)
            consume(qi - 2, *buf_a)
            produce(qi, *buf_a, True)
            consume(qi - 1, *buf_b)
            consume(qi, *buf_a)

        o_t = acc_ref[...] / l_ref[...]
        o = jnp.transpose(o_t[:, 0:blk] - lam * o_t[:, blk:2 * blk])
        ms = jnp.mean(o * o, axis=-1, keepdims=True)
        y = (o * lax.rsqrt(ms + EPS)) * g_ref[...] * (1.0 - lam_init)
        az = az_ref[rows, :].astype(F32)
        o_ref[rows, :] = (y * (az * _sigmoid(az))).astype(BF16)
        return carry

    lax.fori_loop(0, nb, qblock, 0)


def _attention(q, k, vt, p, lambda_qk, attn_norm_g, *, bsz, seq, blk, lam_init):
    m = bsz * seq
    nb = seq // blk

    return pl.pallas_call(
        functools.partial(_attn_kernel, blk=blk, lam_init=lam_init),
        grid=(bsz * HEADS,),
        in_specs=[
            pl.BlockSpec((None, nb, LANES, blk), lambda bh: (bh % HEADS, bh // HEADS, 0, 0)),
            pl.BlockSpec((None, seq, LANES), lambda bh: (bh % HEADS, bh // HEADS, 0)),
            pl.BlockSpec((None, nb, V_DIM, blk), lambda bh: (bh % HEADS, bh // HEADS, 0, 0)),
            pl.BlockSpec((None, seq, LANES), lambda bh: (SLAB_AZ + bh % HEADS, bh // HEADS, 0)),
            pl.BlockSpec((4, QK_DIM), lambda bh: (0, 0)),
            pl.BlockSpec((1, V_DIM), lambda bh: (0, 0)),
        ],
        out_specs=pl.BlockSpec((None, seq, LANES), lambda bh: (bh % HEADS, bh // HEADS, 0)),
        out_shape=jax.ShapeDtypeStruct((HEADS, m, LANES), BF16),
        scratch_shapes=[
            pltpu.VMEM((LANES, 2 * blk), BF16),
            pltpu.VMEM((blk, 2 * blk), F32),
            pltpu.VMEM((blk, 2 * blk), F32),
            pltpu.VMEM((1, 2 * blk), F32),
            pltpu.VMEM((1, 2 * blk), F32),
            pltpu.VMEM((1, 2 * blk), F32),
            pltpu.VMEM((1, 2 * blk), F32),
            pltpu.VMEM((V_DIM, 2 * blk), F32),
        ],
        compiler_params=pltpu.CompilerParams(
            dimension_semantics=("arbitrary",), vmem_limit_bytes=VMEM_LIMIT),
        name="diffattn",
    )(q, k, vt, p, lambda_qk, attn_norm_g)


def _log_sigmoid(x):
    return jnp.minimum(x, 0.0) - jnp.log(1.0 + jnp.exp(-jnp.abs(x)))


def _mlstm_kernel(qk_ref, v_ref, bo_ref, bz_ref, gate_ref, gbias_ref, cw_ref, cb_ref, ng_ref,
                  o_ref, tail_ref, c_ref, n_ref, m_ref, *, chunk):
    L = chunk
    n_pair = HEADS // 2

    @pl.when(pl.program_id(1) == 0)
    def _():
        tail_ref[...] = jnp.zeros(tail_ref.shape, F32)
        c_ref[...] = jnp.zeros(c_ref.shape, F32)
        n_ref[...] = jnp.zeros(n_ref.shape, F32)
        m_ref[...] = jnp.full(m_ref.shape, NEG_INIT, F32)

    conv = []
    for sl in range(2 * n_pair):
        x = qk_ref[sl].astype(F32)
        tail_ref[sl, SUBLANES:SUBLANES + L, :] = x
        y = x * cw_ref[(CONV_WIDTH - 1) * 8 + sl: (CONV_WIDTH - 1) * 8 + sl + 1, :] + cb_ref[sl:sl + 1, :]
        for d in range(1, CONV_WIDTH):
            j = CONV_WIDTH - 1 - d
            y = y + tail_ref[sl, SUBLANES - d:SUBLANES - d + L, :] * cw_ref[j * 8 + sl: j * 8 + sl + 1, :]
        tail_ref[sl, 0:SUBLANES, :] = x[L - SUBLANES:L]
        conv.append(y * _sigmoid(y))
    lane = lax.broadcasted_iota(jnp.int32, (L, LANES), 1)
    lo_half = lane < QK_DIM

    g = gate_ref[...] + gbias_ref[...]
    ls = _log_sigmoid(g)
    r_i = lax.broadcasted_iota(jnp.int32, (L, L), 0)
    c_i = lax.broadcasted_iota(jnp.int32, (L, L), 1)
    tri = jnp.where(c_i <= r_i, 1.0, 0.0).astype(BF16)
    ls_hi, ls_lo = _split_hi_lo(ls)
    bcum = (jnp.dot(tri, ls_hi, preferred_element_type=F32)
            + jnp.dot(tri, ls_lo, preferred_element_type=F32))
    g_t = jnp.transpose(g)
    b_t = jnp.transpose(bcum)
    causal_t = r_i <= c_i
    lane1 = lax.broadcasted_iota(jnp.int32, (1, LANES), 1)
    nt = (((1,), (1,)), ((), ()))

    for pr in range(n_pair):
        q_bf = (conv[pr] * (QK_DIM ** -0.5)).astype(BF16)
        k_bf = conv[n_pair + pr].astype(BF16)
        zero = jnp.zeros_like(q_bf)
        q_stack = jnp.concatenate([jnp.where(lo_half, q_bf, zero), jnp.where(lo_half, zero, q_bf)], axis=0)
        k_halves = (jnp.where(lo_half, k_bf, zero), jnp.where(lo_half, zero, k_bf))
        s_pair = lax.dot_general(k_bf, q_stack, nt, preferred_element_type=F32)
        ct_pair = c_ref[pr]
        inter_pair = lax.dot_general(ct_pair.astype(BF16), q_stack, nt, preferred_element_type=F32)
        n_pair_row = n_ref[pr]
        qn_pair = lax.dot_general(jnp.broadcast_to(n_pair_row, (2 * SUBLANES, LANES)).astype(BF16), q_stack, nt,
                                  preferred_element_type=F32)[0:1]

        a_vals, c_upd, n_upd = [], [], []
        for half in range(2):
            h = 2 * pr + half
            cols = slice(half * L, (half + 1) * L)
            ccol = g[:, h: h + 1] - bcum[:, HEADS + h: HEADS + h + 1]
            brow = b_t[HEADS + h: HEADS + h + 1, :]
            irow = g_t[h: h + 1, :]
            blast = brow[:, L - 1: L]
            m_prev = m_ref[h]

            d_t = jnp.where(causal_t, ccol + brow, -jnp.inf)
            inter_log = brow + m_prev
            m_t = jnp.maximum(inter_log, jnp.max(d_t, axis=0, keepdims=True))
            sc = s_pair[:, cols] * jnp.exp(d_t - m_t)
            inter_w = jnp.exp(inter_log - m_t)
            v_t = jnp.transpose(v_ref[h].astype(F32))
            num = (jnp.dot(v_t.astype(BF16), sc.astype(BF16), preferred_element_type=F32)
                   + inter_w * inter_pair[:, cols])
            den = jnp.sum(sc, axis=0, keepdims=True) + inter_w * qn_pair[:, cols]
            h_t = num / jnp.maximum(jnp.abs(den), jnp.exp(-m_t))

            ms = jnp.mean(h_t * h_t, axis=0, keepdims=True)
            y = jnp.transpose(h_t * lax.rsqrt(ms + EPS)) * ng_ref[...]
            bo = bo_ref[h].astype(F32)
            bz = bz_ref[h].astype(F32)
            o_ref[h] = (_sigmoid(bo) * y * (bz * _sigmoid(bz))).astype(BF16)

            w_log = blast - brow + irow
            m_new = jnp.maximum(blast + m_prev, jnp.max(w_log, axis=-1, keepdims=True))
            a_vals.append(jnp.exp(blast + m_prev - m_new))
            w_row = jnp.exp(w_log - m_new)
            c_upd.append(jnp.dot((v_t * w_row).astype(BF16), k_halves[half], preferred_element_type=F32))
            n_upd.append(jnp.dot(jnp.broadcast_to(w_row, (2 * SUBLANES, L)).astype(BF16), k_halves[half],
                                 preferred_element_type=F32)[0:1])
            m_ref[h] = m_new

        a_row = jnp.where(lane1 < QK_DIM, a_vals[0], a_vals[1])
        c_ref[pr] = a_row * ct_pair + c_upd[0] + c_upd[1]
        n_ref[pr] = a_row * n_pair_row + n_upd[0] + n_upd[1]


def _mlstm(p, gates, gate_bias, conv_w, conv_b, norm_g, *, bsz, seq, chunk):
    m = bsz * seq
    nc = seq // chunk

    def slab_map(group):
        return lambda b, c: (group, b * nc + c, 0)

    return pl.pallas_call(
        functools.partial(_mlstm_kernel, chunk=chunk),
        grid=(bsz, nc),
        in_specs=[
            pl.BlockSpec((HEADS, chunk, LANES), slab_map(SLAB_BQK // HEADS)),
            pl.BlockSpec((HEADS, chunk, LANES), slab_map(SLAB_BV // HEADS)),
            pl.BlockSpec((HEADS, chunk, LANES), slab_map(SLAB_BO // HEADS)),
            pl.BlockSpec((HEADS, chunk, LANES), slab_map(SLAB_BZ // HEADS)),
            pl.BlockSpec((chunk, LANES), lambda b, c: (b * nc + c, 0)),
            pl.BlockSpec((1, LANES), lambda b, c: (0, 0)),
            pl.BlockSpec((CONV_WIDTH * 8, LANES), lambda b, c: (0, 0)),
            pl.BlockSpec((8, LANES), lambda b, c: (0, 0)),
            pl.BlockSpec((1, V_DIM), lambda b, c: (0, 0)),
        ],
        out_specs=pl.BlockSpec((HEADS, chunk, LANES), lambda b, c: (0, b * nc + c, 0)),
        out_shape=jax.ShapeDtypeStruct((HEADS, m, LANES), BF16),
        scratch_shapes=[
            pltpu.VMEM((HEADS, SUBLANES + chunk, LANES), F32),
            pltpu.VMEM((HEADS // 2, V_DIM, LANES), F32),
            pltpu.VMEM((HEADS // 2, 1, LANES), F32),
            pltpu.VMEM((HEADS, 1, 1), F32),
        ],
        compiler_params=pltpu.CompilerParams(
            dimension_semantics=("arbitrary", "arbitrary"), vmem_limit_bytes=VMEM_LIMIT),
        name="mlstm",
    )(p, p, p, p, gates, gate_bias, conv_w, conv_b, norm_g)


def _outproj_kernel(oa_ref, hb_ref, ga_ref, gb_ref, x_ref, wa_ref, wb_ref, wo_ref, o_ref):
    def cat(ref):
        return jnp.concatenate([ref[h] for h in range(HEADS)], axis=1)

    ya = jnp.dot(cat(oa_ref), wa_ref[...], preferred_element_type=F32)
    yb = jnp.dot(cat(hb_ref), wb_ref[...], preferred_element_type=F32)
    u = _sigmoid(cat(ga_ref).astype(F32)) * ya + _sigmoid(cat(gb_ref).astype(F32)) * yb
    o_ref[...] = x_ref[...] + jnp.dot(u.astype(BF16), wo_ref[...], preferred_element_type=F32)


def _outproj(oa, hb, p, x2, wa, wb, wo, *, tm):
    m = x2.shape[0]
    w_spec = pl.BlockSpec((D_MODEL, D_MODEL), lambda i: (0, 0))
    return pl.pallas_call(
        _outproj_kernel,
        grid=(m // tm,),
        in_specs=[
            pl.BlockSpec((HEADS, tm, LANES), lambda i: (0, i, 0)),
            pl.BlockSpec((HEADS, tm, LANES), lambda i: (0, i, 0)),
            pl.BlockSpec((HEADS, tm, LANES), lambda i: (SLAB_GA // HEADS, i, 0)),
            pl.BlockSpec((HEADS, tm, LANES), lambda i: (SLAB_GB // HEADS, i, 0)),
            pl.BlockSpec((tm, D_MODEL), lambda i: (i, 0)),
            w_spec, w_spec, w_spec,
        ],
        out_specs=pl.BlockSpec((tm, D_MODEL), lambda i: (i, 0)),
        out_shape=jax.ShapeDtypeStruct((m, D_MODEL), F32),
        compiler_params=pltpu.CompilerParams(
            dimension_semantics=("arbitrary",), vmem_limit_bytes=VMEM_LIMIT),
        name="outproj",
    )(oa, hb, p, p, x2, wa, wb, wo)


def _reorder_w_in(w_in):
    idx = [0]
    for s in _REF_SPLITS:
        idx.append(idx[-1] + s)
    parts = [w_in[:, idx[i]:idx[i + 1]] for i in range(len(_REF_SPLITS))]
    aq, ak, av, az, bqk, bv, bi, bf, bo, bz, ga, gb = parts
    w_main = jnp.concatenate([aq, ak, av, az, bqk, bv, bo, bz, ga, gb], axis=1).astype(BF16)
    w_gate = jnp.concatenate([bi, bf, jnp.zeros((D_MODEL, LANES - 2 * HEADS), w_in.dtype)], axis=1).astype(BF16)
    return w_main, w_gate


def _pick(n, pref):
    return pref if n % pref == 0 else n


def kernel(x, positions, norm_g, w_in, q_norm_g, k_norm_g, lambda_qk, attn_norm_g, w_out_a,
           conv_w, conv_b, igate_b, fgate_b, mlstm_norm_g, w_out_b, w_o):
    bsz, seq, _ = x.shape
    m = bsz * seq
    depth = w_in.shape[0]
    tm = _pick(m, 1024)
    blk = _pick(seq, 512)
    chunk = _pick(seq, 256)

    inv_freq = ROPE_THETA ** (-jnp.arange(0, QK_DIM, 2, dtype=F32) / QK_DIM)
    ang = positions.astype(F32).reshape(m, 1) * inv_freq[None, :]
    cos, sin = jnp.cos(ang), jnp.sin(ang)
    cos4 = jnp.concatenate([cos, cos, cos, cos], axis=1)
    sin4 = jnp.concatenate([-sin, sin, -sin, sin], axis=1)
    grp = jnp.arange(2 * LANES) // QK_DIM
    ones_blk = (grp[:, None] == grp[None, :]).astype(BF16)

    x2 = x.reshape(m, D_MODEL)
    for l in range(depth):
        lam_init = 0.8 - 0.6 * math.exp(-0.3 * l)
        w_main, w_gate = _reorder_w_in(w_in[l])
        p, gates = _inproj(x2, norm_g[l][None, :], w_main, w_gate, tm=tm, tn=1024)

        gq = jnp.tile(q_norm_g[l], 2) * (QK_DIM ** -0.5 * LOG2_E)
        gk = jnp.tile(k_norm_g[l], 2)
        gains = jnp.concatenate([gq, gk])[None, :]
        q, k, vt = _attnprep(p, cos4, sin4, gains, ones_blk, tm=tm, blk=blk)
        oa = _attention(q, k, vt, p, lambda_qk[l], attn_norm_g[l][None, :],
                        bsz=bsz, seq=seq, blk=blk, lam_init=lam_init)

        gate_bias = jnp.concatenate([igate_b[l], fgate_b[l], jnp.zeros((LANES - 2 * HEADS,), F32)])[None, :]
        cw = conv_w[l].reshape(CONV_WIDTH * 8, LANES)
        cb = conv_b[l].reshape(8, LANES)
        hb = _mlstm(p, gates, gate_bias, cw, cb, mlstm_norm_g[l][None, :], bsz=bsz, seq=seq, chunk=chunk)

        x2 = _outproj(oa, hb, p, x2, w_out_a[l].astype(BF16), w_out_b[l].astype(BF16),
                      w_o[l].astype(BF16), tm=_pick(m, 512))
    return x2.reshape(bsz, seq, D_MODEL)
```

```python
import functools
import math

import jax
import jax.numpy as jnp
from jax import lax
from jax.experimental import pallas as pl
from jax.experimental.pallas import tpu as pltpu

F32 = jnp.float32
BF16 = jnp.bfloat16

D_MODEL = 1024
HEADS = 8
QK_DIM = 64
V_DIM = 128
CONV_WIDTH = 4
ROPE_THETA = 10000.0
EPS = 1e-6
NEG_INIT = -1e30
LOG2_E = math.log2(math.e)
LANES = 128
SUBLANES = 8
VMEM_LIMIT = 48 * 1024 * 1024

SLAB_AQ, SLAB_AK, SLAB_AV, SLAB_AZ = 0, 8, 16, 24
SLAB_BQK, SLAB_BV, SLAB_BO, SLAB_BZ, SLAB_GA, SLAB_GB = 32, 40, 48, 56, 64, 72
N_SLABS = 80
N_MAIN = N_SLABS * LANES

_REF_SPLITS = (1024, 1024, 1024, 1024, 1024, 1024, 8, 8, 1024, 1024, 1024, 1024)


def _split_hi_lo(x):
    hi = x.astype(BF16)
    lo = (x - hi.astype(F32)).astype(BF16)
    return hi, lo


def _sigmoid(x):
    return 1.0 / (1.0 + jnp.exp2(x * (-LOG2_E)))


def _inproj_kernel(x_ref, g_ref, w_ref, wg_ref, p_ref, gate_ref, h_ref, *, n_sub):
    @pl.when(pl.program_id(1) == 0)
    def _():
        x = x_ref[...]
        ms = jnp.mean(x * x, axis=-1, keepdims=True)
        h = ((x * lax.rsqrt(ms + EPS)) * g_ref[...]).astype(BF16)
        h_ref[...] = h
        gate_ref[...] = jnp.dot(h, wg_ref[...], preferred_element_type=F32)

    acc = jnp.dot(h_ref[...], w_ref[...], preferred_element_type=F32)
    for c in range(n_sub):
        p_ref[c] = acc[:, c * LANES:(c + 1) * LANES].astype(BF16)


def _inproj(x2, norm_g, w_main, w_gate, *, layer, tm, tn):
    m = x2.shape[0]
    n_sub = tn // LANES
    return pl.pallas_call(
        functools.partial(_inproj_kernel, n_sub=n_sub),
        grid=(m // tm, N_MAIN // tn),
        in_specs=[
            pl.BlockSpec((tm, D_MODEL), lambda i, j: (i, 0)),
            pl.BlockSpec((1, D_MODEL), lambda i, j: (0, 0)),
            pl.BlockSpec((None, D_MODEL, tn), lambda i, j: (layer, 0, j)),
            pl.BlockSpec((None, D_MODEL, LANES), lambda i, j: (layer, 0, 0)),
        ],
        out_specs=[
            pl.BlockSpec((n_sub, tm, LANES), lambda i, j: (j, i, 0)),
            pl.BlockSpec((tm, LANES), lambda i, j: (i, 0)),
        ],
        out_shape=[
            jax.ShapeDtypeStruct((N_SLABS, m, LANES), BF16),
            jax.ShapeDtypeStruct((m, LANES), F32),
        ],
        scratch_shapes=[pltpu.VMEM((tm, D_MODEL), BF16)],
        compiler_params=pltpu.CompilerParams(
            dimension_semantics=("arbitrary", "arbitrary"), vmem_limit_bytes=VMEM_LIMIT),
        name="inproj",
    )(x2, norm_g, w_main, w_gate)


def _attnprep_kernel(q_ref, k_ref, v_ref, cos_ref, sin_ref, g_ref, ones_ref, qt_ref, ko_ref, vt_ref,
                     *, blk, n_sub):
    x = jnp.concatenate([q_ref[...], k_ref[...]], axis=1).astype(F32)
    hi, lo = _split_hi_lo(x * x)
    ones = ones_ref[...]
    ss = (jnp.dot(hi, ones, preferred_element_type=F32)
          + jnp.dot(lo, ones, preferred_element_type=F32))
    y = (x * lax.rsqrt(ss * (1.0 / QK_DIM) + EPS)) * g_ref[...]
    cos = cos_ref[...]
    sin = sin_ref[...]
    lane = lax.broadcasted_iota(jnp.int32, cos.shape, 1)
    half = QK_DIM // 2
    roped = []
    for side in range(2):
        ys = y[:, side * LANES:(side + 1) * LANES]
        partner = jnp.where((lane & half) == 0,
                            pltpu.roll(ys, LANES - half, axis=1),
                            pltpu.roll(ys, half, axis=1))
        roped.append(ys * cos + partner * sin)
    ko_ref[...] = roped[1].astype(BF16)
    for c in range(n_sub):
        rows = slice(c * blk, (c + 1) * blk)
        qt_ref[c] = jnp.transpose(roped[0][rows]).astype(BF16)
        vt_ref[c] = jnp.transpose(v_ref[rows, :].astype(F32)).astype(BF16)


def _attnprep(p, cos4, sin4, gains, ones_blk, *, tm, blk):
    m = p.shape[1]
    n_sub = tm // blk
    return pl.pallas_call(
        functools.partial(_attnprep_kernel, blk=blk, n_sub=n_sub),
        grid=(m // tm, HEADS),
        in_specs=[
            pl.BlockSpec((None, tm, LANES), lambda i, h: (SLAB_AQ + h, i, 0)),
            pl.BlockSpec((None, tm, LANES), lambda i, h: (SLAB_AK + h, i, 0)),
            pl.BlockSpec((None, tm, LANES), lambda i, h: (SLAB_AV + h, i, 0)),
            pl.BlockSpec((tm, LANES), lambda i, h: (i, 0)),
            pl.BlockSpec((tm, LANES), lambda i, h: (i, 0)),
            pl.BlockSpec((1, 2 * LANES), lambda i, h: (0, 0)),
            pl.BlockSpec((2 * LANES, 2 * LANES), lambda i, h: (0, 0)),
        ],
        out_specs=[
            pl.BlockSpec((None, n_sub, LANES, blk), lambda i, h: (h, i, 0, 0)),
            pl.BlockSpec((None, tm, LANES), lambda i, h: (h, i, 0)),
            pl.BlockSpec((None, n_sub, V_DIM, blk), lambda i, h: (h, i, 0, 0)),
        ],
        out_shape=[
            jax.ShapeDtypeStruct((HEADS, m // blk, LANES, blk), BF16),
            jax.ShapeDtypeStruct((HEADS, m, LANES), BF16),
            jax.ShapeDtypeStruct((HEADS, m // blk, V_DIM, blk), BF16),
        ],
        compiler_params=pltpu.CompilerParams(
            dimension_semantics=("arbitrary", "arbitrary"), vmem_limit_bytes=VMEM_LIMIT),
        name="attnprep",
    )(p, p, p, cos4, sin4, gains, ones_blk)


def _attn_kernel(q_ref, k_ref, vt_ref, az_ref, lam_ref, g_ref, o_ref,
                 qs_ref, sa_ref, sb_ref, ca_ref, cb_ref, m_ref, l_ref, acc_ref, *, blk, lam_init):
    nb = q_ref.shape[0]

    def produce(kc, s_ref, c_ref, masked):
        k = k_ref[pl.ds(pl.multiple_of(kc * blk, blk), blk), :]
        s = jnp.dot(k, qs_ref[...], preferred_element_type=F32)
        if masked:
            kpos = lax.broadcasted_iota(jnp.int32, s.shape, 0)
            qpos = lax.broadcasted_iota(jnp.int32, s.shape, 1) & (blk - 1)
            s = jnp.where(kpos <= qpos, s, -jnp.inf)
        s_ref[...] = s
        c_ref[...] = jnp.max(s, axis=0, keepdims=True)

    def consume(kc, s_ref, c_ref):
        m = m_ref[...]
        m_new = jnp.maximum(m, c_ref[...])
        alpha = jnp.exp2(m - m_new)
        p = jnp.exp2(s_ref[...] - m_new)
        l_ref[...] = alpha * l_ref[...] + jnp.sum(p, axis=0, keepdims=True)
        pv = jnp.dot(vt_ref[kc], p.astype(BF16), preferred_element_type=F32)
        acc_ref[...] = acc_ref[...] * alpha + pv
        m_ref[...] = m_new

    buf_a = (sa_ref, ca_ref)
    buf_b = (sb_ref, cb_ref)

    def two_chunks(j):
        produce(j + 1, *buf_b, False)
        consume(j, *buf_a)
        produce(j + 2, *buf_a, False)
        consume(j + 1, *buf_b)

    def pair(t, carry):
        two_chunks(2 * t)
        return carry

    def quad(t, carry):
        two_chunks(4 * t)
        two_chunks(4 * t + 2)
        return carry

    lq = lam_ref[...]
    lam = (jnp.exp(jnp.sum(lq[0:1] * lq[1:2], axis=-1, keepdims=True))
           - jnp.exp(jnp.sum(lq[2:3] * lq[3:4], axis=-1, keepdims=True)) + lam_init)

    def qblock(qi, carry):
        rows = pl.ds(pl.multiple_of(qi * blk, blk), blk)
        q_t = q_ref[qi]
        dim = lax.broadcasted_iota(jnp.int32, q_t.shape, 0)
        zero = jnp.zeros_like(q_t)
        qs_ref[:, 0:blk] = jnp.where(dim < QK_DIM, q_t, zero)
        qs_ref[:, blk:2 * blk] = jnp.where(dim >= QK_DIM, q_t, zero)
        acc_ref[...] = jnp.zeros(acc_ref.shape, F32)
        m_ref[...] = jnp.full(m_ref.shape, -jnp.inf, F32)
        l_ref[...] = jnp.zeros(l_ref.shape, F32)

        @pl.when(qi == 0)
        def _():
            produce(0, *buf_a, True)
            consume(0, *buf_a)

        @pl.when(qi > 0)
        def _():
            produce(0, *buf_a, False)

        n_pairs = jnp.maximum(qi - 1, 0) // 2
        n_quads = n_pairs // 2
        lax.fori_loop(0, n_quads, quad, 0)
        lax.fori_loop(2 * n_quads, n_pairs, pair, 0)
        rest = qi - 2 * n_pairs

        @pl.when(rest == 1)
        def _():
            produce(qi, *buf_b, True)
            consume(qi - 1, *buf_a)
            consume(qi, *buf_b)

        @pl.when(rest == 2)
        def _():
            produce(qi - 1, *buf_b, False)
            consume(qi - 2, *buf_a)
            produce(qi, *buf_a, True)
            consume(qi - 1, *buf_b)
            consume(qi, *buf_a)

        o_t = acc_ref[...] / l_ref[...]
        o = jnp.transpose(o_t[:, 0:blk] - lam * o_t[:, blk:2 * blk])
        ms = jnp.mean(o * o, axis=-1, keepdims=True)
        y = (o * lax.rsqrt(ms + EPS)) * g_ref[...] * (1.0 - lam_init)
        az = az_ref[rows, :].astype(F32)
        o_ref[rows, :] = (y * (az * _sigmoid(az))).astype(BF16)
        return carry

    lax.fori_loop(0, nb, qblock, 0)


def _attention(q, k, vt, p, lambda_qk, attn_norm_g, *, bsz, seq, blk, lam_init):
    m = bsz * seq
    nb = seq // blk

    return pl.pallas_call(
        functools.partial(_attn_kernel, blk=blk, lam_init=lam_init),
        grid=(bsz * HEADS,),
        in_specs=[
            pl.BlockSpec((None, nb, LANES, blk), lambda bh: (bh % HEADS, bh // HEADS, 0, 0)),
            pl.BlockSpec((None, seq, LANES), lambda bh: (bh % HEADS, bh // HEADS, 0)),
            pl.BlockSpec((None, nb, V_DIM, blk), lambda bh: (bh % HEADS, bh // HEADS, 0, 0)),
            pl.BlockSpec((None, seq, LANES), lambda bh: (SLAB_AZ + bh % HEADS, bh // HEADS, 0)),
            pl.BlockSpec((4, QK_DIM), lambda bh: (0, 0)),
            pl.BlockSpec((1, V_DIM), lambda bh: (0, 0)),
        ],
        out_specs=pl.BlockSpec((None, seq, LANES), lambda bh: (bh % HEADS, bh // HEADS, 0)),
        out_shape=jax.ShapeDtypeStruct((HEADS, m, LANES), BF16),
        scratch_shapes=[
            pltpu.VMEM((LANES, 2 * blk), BF16),
            pltpu.VMEM((blk, 2 * blk), F32),
            pltpu.VMEM((blk, 2 * blk), F32),
            pltpu.VMEM((1, 2 * blk), F32),
            pltpu.VMEM((1, 2 * blk), F32),
            pltpu.VMEM((1, 2 * blk), F32),
            pltpu.VMEM((1, 2 * blk), F32),
            pltpu.VMEM((V_DIM, 2 * blk), F32),
        ],
        compiler_params=pltpu.CompilerParams(
            dimension_semantics=("arbitrary",), vmem_limit_bytes=VMEM_LIMIT),
        name="diffattn",
    )(q, k, vt, p, lambda_qk, attn_norm_g)


def _log_sigmoid(x):
    return jnp.minimum(x, 0.0) - jnp.log(1.0 + jnp.exp(-jnp.abs(x)))


def _mlstm_kernel(qk_ref, v_ref, bo_ref, bz_ref, gate_ref, gbias_ref, cw_ref, cb_ref, ng_ref,
                  o_ref, tail_ref, c_ref, n_ref, m_ref, *, chunk):
    L = chunk
    n_pair = HEADS // 2

    @pl.when(pl.program_id(1) == 0)
    def _():
        tail_ref[...] = jnp.zeros(tail_ref.shape, F32)
        c_ref[...] = jnp.zeros(c_ref.shape, F32)
        n_ref[...] = jnp.zeros(n_ref.shape, F32)
        m_ref[...] = jnp.full(m_ref.shape, NEG_INIT, F32)

    conv = []
    for sl in range(2 * n_pair):
        x = qk_ref[sl].astype(F32)
        tail_ref[sl, SUBLANES:SUBLANES + L, :] = x
        y = x * cw_ref[(CONV_WIDTH - 1) * 8 + sl: (CONV_WIDTH - 1) * 8 + sl + 1, :] + cb_ref[sl:sl + 1, :]
        for d in range(1, CONV_WIDTH):
            j = CONV_WIDTH - 1 - d
            y = y + tail_ref[sl, SUBLANES - d:SUBLANES - d + L, :] * cw_ref[j * 8 + sl: j * 8 + sl + 1, :]
        tail_ref[sl, 0:SUBLANES, :] = x[L - SUBLANES:L]
        conv.append(y * _sigmoid(y))
    lane = lax.broadcasted_iota(jnp.int32, (L, LANES), 1)
    lo_half = lane < QK_DIM

    g = gate_ref[...] + gbias_ref[...]
    ls = _log_sigmoid(g)
    r_i = lax.broadcasted_iota(jnp.int32, (L, L), 0)
    c_i = lax.broadcasted_iota(jnp.int32, (L, L), 1)
    tri = jnp.where(c_i <= r_i, 1.0, 0.0).astype(BF16)
    ls_hi, ls_lo = _split_hi_lo(ls)
    bcum = (jnp.dot(tri, ls_hi, preferred_element_type=F32)
            + jnp.dot(tri, ls_lo, preferred_element_type=F32))
    g_t = jnp.transpose(g)
    b_t = jnp.transpose(bcum)
    causal_t = r_i <= c_i
    lane1 = lax.broadcasted_iota(jnp.int32, (1, LANES), 1)
    nt = (((1,), (1,)), ((), ()))

    for pr in range(n_pair):
        q_bf = (conv[pr] * (QK_DIM ** -0.5)).astype(BF16)
        k_bf = conv[n_pair + pr].astype(BF16)
        zero = jnp.zeros_like(q_bf)
        q_stack = jnp.concatenate([jnp.where(lo_half, q_bf, zero), jnp.where(lo_half, zero, q_bf)], axis=0)
        k_halves = (jnp.where(lo_half, k_bf, zero), jnp.where(lo_half, zero, k_bf))
        s_pair = lax.dot_general(k_bf, q_stack, nt, preferred_element_type=F32)
        ct_pair = c_ref[pr]
        inter_pair = lax.dot_general(ct_pair.astype(BF16), q_stack, nt, preferred_element_type=F32)
        n_pair_row = n_ref[pr]
        qn_pair = lax.dot_general(jnp.broadcast_to(n_pair_row, (2 * SUBLANES, LANES)).astype(BF16), q_stack, nt,
                                  preferred_element_type=F32)[0:1]

        a_vals, c_upd, n_upd = [], [], []
        for half in range(2):
            h = 2 * pr + half
            cols = slice(half * L, (half + 1) * L)
            ccol = g[:, h: h + 1] - bcum[:, HEADS + h: HEADS + h + 1]
            brow = b_t[HEADS + h: HEADS + h + 1, :]
            irow = g_t[h: h + 1, :]
            blast = brow[:, L - 1: L]
            m_prev = m_ref[h]

            d_t = jnp.where(causal_t, ccol + brow, -jnp.inf)
            inter_log = brow + m_prev
            m_t = jnp.maximum(inter_log, jnp.max(d_t, axis=0, keepdims=True))
            sc = s_pair[:, cols] * jnp.exp(d_t - m_t)
            inter_w = jnp.exp(inter_log - m_t)
            v_t = jnp.transpose(v_ref[h].astype(F32))
            num = (jnp.dot(v_t.astype(BF16), sc.astype(BF16), preferred_element_type=F32)
                   + inter_w * inter_pair[:, cols])
            den = jnp.sum(sc, axis=0, keepdims=True) + inter_w * qn_pair[:, cols]
            h_t = num / jnp.maximum(jnp.abs(den), jnp.exp(-m_t))

            ms = jnp.mean(h_t * h_t, axis=0, keepdims=True)
            y = jnp.transpose(h_t * lax.rsqrt(ms + EPS)) * ng_ref[...]
            bo = bo_ref[h].astype(F32)
            bz = bz_ref[h].astype(F32)
            o_ref[h] = (_sigmoid(bo) * y * (bz * _sigmoid(bz))).astype(BF16)

            w_log = blast - brow + irow
            m_new = jnp.maximum(blast + m_prev, jnp.max(w_log, axis=-1, keepdims=True))
            a_vals.append(jnp.exp(blast + m_prev - m_new))
            w_row = jnp.exp(w_log - m_new)
            c_upd.append(jnp.dot((v_t * w_row).astype(BF16), k_halves[half], preferred_element_type=F32))
            n_upd.append(jnp.dot(jnp.broadcast_to(w_row, (2 * SUBLANES, L)).astype(BF16), k_halves[half],
                                 preferred_element_type=F32)[0:1])
            m_ref[h] = m_new

        a_row = jnp.where(lane1 < QK_DIM, a_vals[0], a_vals[1])
        c_ref[pr] = a_row * ct_pair + c_upd[0] + c_upd[1]
        n_ref[pr] = a_row * n_pair_row + n_upd[0] + n_upd[1]


def _mlstm(p, gates, gate_bias, conv_w, conv_b, norm_g, *, bsz, seq, chunk):
    m = bsz * seq
    nc = seq // chunk

    def slab_map(group):
        return lambda b, c: (group, b * nc + c, 0)

    return pl.pallas_call(
        functools.partial(_mlstm_kernel, chunk=chunk),
        grid=(bsz, nc),
        in_specs=[
            pl.BlockSpec((HEADS, chunk, LANES), slab_map(SLAB_BQK // HEADS)),
            pl.BlockSpec((HEADS, chunk, LANES), slab_map(SLAB_BV // HEADS)),
            pl.BlockSpec((HEADS, chunk, LANES), slab_map(SLAB_BO // HEADS)),
            pl.BlockSpec((HEADS, chunk, LANES), slab_map(SLAB_BZ // HEADS)),
            pl.BlockSpec((chunk, LANES), lambda b, c: (b * nc + c, 0)),
            pl.BlockSpec((1, LANES), lambda b, c: (0, 0)),
            pl.BlockSpec((CONV_WIDTH * 8, LANES), lambda b, c: (0, 0)),
            pl.BlockSpec((8, LANES), lambda b, c: (0, 0)),
            pl.BlockSpec((1, V_DIM), lambda b, c: (0, 0)),
        ],
        out_specs=pl.BlockSpec((HEADS, chunk, LANES), lambda b, c: (0, b * nc + c, 0)),
        out_shape=jax.ShapeDtypeStruct((HEADS, m, LANES), BF16),
        scratch_shapes=[
            pltpu.VMEM((HEADS, SUBLANES + chunk, LANES), F32),
            pltpu.VMEM((HEADS // 2, V_DIM, LANES), F32),
            pltpu.VMEM((HEADS // 2, 1, LANES), F32),
            pltpu.VMEM((HEADS, 1, 1), F32),
        ],
        compiler_params=pltpu.CompilerParams(
            dimension_semantics=("arbitrary", "arbitrary"), vmem_limit_bytes=VMEM_LIMIT),
        name="mlstm",
    )(p, p, p, p, gates, gate_bias, conv_w, conv_b, norm_g)


def _outproj_kernel(oa_ref, hb_ref, ga_ref, gb_ref, x_ref, wa_ref, wb_ref, wo_ref, o_ref):
    def cat(ref):
        return jnp.concatenate([ref[h] for h in range(HEADS)], axis=1)

    ya = jnp.dot(cat(oa_ref), wa_ref[...], preferred_element_type=F32)
    yb = jnp.dot(cat(hb_ref), wb_ref[...], preferred_element_type=F32)
    u = _sigmoid(cat(ga_ref).astype(F32)) * ya + _sigmoid(cat(gb_ref).astype(F32)) * yb
    o_ref[...] = x_ref[...] + jnp.dot(u.astype(BF16), wo_ref[...], preferred_element_type=F32)


def _outproj(oa, hb, p, x2, wa, wb, wo, *, layer, tm):
    m = x2.shape[0]
    w_spec = pl.BlockSpec((None, D_MODEL, D_MODEL), lambda i: (layer, 0, 0))
    return pl.pallas_call(
        _outproj_kernel,
        grid=(m // tm,),
        in_specs=[
            pl.BlockSpec((HEADS, tm, LANES), lambda i: (0, i, 0)),
            pl.BlockSpec((HEADS, tm, LANES), lambda i: (0, i, 0)),
            pl.BlockSpec((HEADS, tm, LANES), lambda i: (SLAB_GA // HEADS, i, 0)),
            pl.BlockSpec((HEADS, tm, LANES), lambda i: (SLAB_GB // HEADS, i, 0)),
            pl.BlockSpec((tm, D_MODEL), lambda i: (i, 0)),
            w_spec, w_spec, w_spec,
        ],
        out_specs=pl.BlockSpec((tm, D_MODEL), lambda i: (i, 0)),
        out_shape=jax.ShapeDtypeStruct((m, D_MODEL), F32),
        compiler_params=pltpu.CompilerParams(
            dimension_semantics=("arbitrary",), vmem_limit_bytes=VMEM_LIMIT),
        name="outproj",
    )(oa, hb, p, p, x2, wa, wb, wo)


def _reorder_w_in(w_in):
    idx = [0]
    for s in _REF_SPLITS:
        idx.append(idx[-1] + s)
    gate0, gate1 = idx[6], idx[8]
    w_main = jnp.concatenate([w_in[:, :, :gate0], w_in[:, :, gate1:]], axis=2).astype(BF16)
    pad = jnp.zeros(w_in.shape[:2] + (LANES - (gate1 - gate0),), w_in.dtype)
    w_gate = jnp.concatenate([w_in[:, :, gate0:gate1], pad], axis=2).astype(BF16)
    return w_main, w_gate


def _pick(n, pref):
    return pref if n % pref == 0 else n


def kernel(x, positions, norm_g, w_in, q_norm_g, k_norm_g, lambda_qk, attn_norm_g, w_out_a,
           conv_w, conv_b, igate_b, fgate_b, mlstm_norm_g, w_out_b, w_o):
    bsz, seq, _ = x.shape
    m = bsz * seq
    depth = w_in.shape[0]
    tm = _pick(m, 1024)
    blk = _pick(seq, 512)
    chunk = _pick(seq, 256)

    inv_freq = ROPE_THETA ** (-jnp.arange(0, QK_DIM, 2, dtype=F32) / QK_DIM)
    ang = positions.astype(F32).reshape(m, 1) * inv_freq[None, :]
    cos, sin = jnp.cos(ang), jnp.sin(ang)
    cos4 = jnp.concatenate([cos, cos, cos, cos], axis=1)
    sin4 = jnp.concatenate([-sin, sin, -sin, sin], axis=1)
    grp = jnp.arange(2 * LANES) // QK_DIM
    ones_blk = (grp[:, None] == grp[None, :]).astype(BF16)

    x2 = x.reshape(m, D_MODEL)
    w_main, w_gate = _reorder_w_in(w_in)
    wa, wb, wo = w_out_a.astype(BF16), w_out_b.astype(BF16), w_o.astype(BF16)
    for l in range(depth):
        lam_init = 0.8 - 0.6 * math.exp(-0.3 * l)
        p, gates = _inproj(x2, norm_g[l][None, :], w_main, w_gate, layer=l, tm=tm, tn=2048)

        gq = jnp.tile(q_norm_g[l], 2) * (QK_DIM ** -0.5 * LOG2_E)
        gk = jnp.tile(k_norm_g[l], 2)
        gains = jnp.concatenate([gq, gk])[None, :]
        q, k, vt = _attnprep(p, cos4, sin4, gains, ones_blk, tm=tm, blk=blk)
        oa = _attention(q, k, vt, p, lambda_qk[l], attn_norm_g[l][None, :],
                        bsz=bsz, seq=seq, blk=blk, lam_init=lam_init)

        gate_bias = jnp.concatenate([igate_b[l], fgate_b[l], jnp.zeros((LANES - 2 * HEADS,), F32)])[None, :]
        cw = conv_w[l].reshape(CONV_WIDTH * 8, LANES)
        cb = conv_b[l].reshape(8, LANES)
        hb = _mlstm(p, gates, gate_bias, cw, cb, mlstm_norm_g[l][None, :], bsz=bsz, seq=seq, chunk=chunk)

        x2 = _outproj(oa, hb, p, x2, wa, wb, wo, layer=l, tm=_pick(m, 512))
    return x2.reshape(bsz, seq, D_MODEL)
```

```python
import functools
import math

import jax
import jax.numpy as jnp
from jax import lax
from jax.experimental import pallas as pl
from jax.experimental.pallas import tpu as pltpu

F32 = jnp.float32
BF16 = jnp.bfloat16

D_MODEL = 1024
HEADS = 8
QK_DIM = 64
V_DIM = 128
ROPE_HALF = QK_DIM // 2
CONV_WIDTH = 4
ROPE_THETA = 10000.0
EPS = 1e-6
NEG_INIT = -1e30
LOG2_E = math.log2(math.e)
LANES = 128
SUBLANES = 8
VMEM_LIMIT = 48 * 1024 * 1024

SLAB_AQ, SLAB_AK, SLAB_AV, SLAB_AZ = 0, 8, 16, 24
SLAB_BQK, SLAB_BV, SLAB_BO, SLAB_BZ, SLAB_GA, SLAB_GB = 32, 40, 48, 56, 64, 72
N_SLABS = 80
N_MAIN = N_SLABS * LANES

_REF_SPLITS = (1024, 1024, 1024, 1024, 1024, 1024, 8, 8, 1024, 1024, 1024, 1024)


def _split_hi_lo(x):
    hi = x.astype(BF16)
    lo = (x - hi.astype(F32)).astype(BF16)
    return hi, lo


def _sigmoid(x):
    return 1.0 / (1.0 + jnp.exp2(x * (-LOG2_E)))


def _inproj_kernel(x_ref, g_ref, w_ref, wg_ref, p_ref, gate_ref, h_ref, *, n_sub):
    @pl.when(pl.program_id(1) == 0)
    def _():
        x = x_ref[...]
        ms = jnp.mean(x * x, axis=-1, keepdims=True)
        h = ((x * lax.rsqrt(ms + EPS)) * g_ref[...]).astype(BF16)
        h_ref[...] = h
        gate_ref[...] = jnp.dot(h, wg_ref[...], preferred_element_type=F32)

    acc = jnp.dot(h_ref[...], w_ref[...], preferred_element_type=F32)
    for c in range(n_sub):
        p_ref[c] = acc[:, c * LANES:(c + 1) * LANES].astype(BF16)


def _inproj(x2, norm_g, w_main, w_gate, *, layer, tm, tn):
    m = x2.shape[0]
    n_sub = tn // LANES
    return pl.pallas_call(
        functools.partial(_inproj_kernel, n_sub=n_sub),
        grid=(m // tm, N_MAIN // tn),
        in_specs=[
            pl.BlockSpec((tm, D_MODEL), lambda i, j: (i, 0)),
            pl.BlockSpec((1, D_MODEL), lambda i, j: (0, 0)),
            pl.BlockSpec((None, D_MODEL, tn), lambda i, j: (layer, 0, j)),
            pl.BlockSpec((None, D_MODEL, LANES), lambda i, j: (layer, 0, 0)),
        ],
        out_specs=[
            pl.BlockSpec((n_sub, tm, LANES), lambda i, j: (j, i, 0)),
            pl.BlockSpec((tm, LANES), lambda i, j: (i, 0)),
        ],
        out_shape=[
            jax.ShapeDtypeStruct((N_SLABS, m, LANES), BF16),
            jax.ShapeDtypeStruct((m, LANES), F32),
        ],
        scratch_shapes=[pltpu.VMEM((tm, D_MODEL), BF16)],
        compiler_params=pltpu.CompilerParams(
            dimension_semantics=("arbitrary", "arbitrary"), vmem_limit_bytes=VMEM_LIMIT),
        name="inproj",
    )(x2, norm_g, w_main, w_gate)


def _attnprep_kernel(q_ref, k_ref, v_ref, cos_ref, sin_ref, g_ref, ones_ref, qt_ref, ko_ref, vt_ref,
                     *, blk, n_sub):
    x = jnp.concatenate([q_ref[...], k_ref[...]], axis=1).astype(F32)
    hi, lo = _split_hi_lo(x * x)
    ones = ones_ref[...]
    ss = (jnp.dot(hi, ones, preferred_element_type=F32)
          + jnp.dot(lo, ones, preferred_element_type=F32))
    y = (x * lax.rsqrt(ss * (1.0 / QK_DIM) + EPS)) * g_ref[...]
    cos = cos_ref[...]
    sin = sin_ref[...]
    roped = []
    for side in range(2):
        ys = y[:, side * LANES:(side + 1) * LANES]
        roped.append(ys * cos + pltpu.roll(ys, LANES // 2, axis=1) * sin)
    ko_ref[...] = roped[1].astype(BF16)
    for c in range(n_sub):
        rows = slice(c * blk, (c + 1) * blk)
        qt_ref[c] = jnp.transpose(roped[0][rows]).astype(BF16)
        vt_ref[c] = jnp.transpose(v_ref[rows, :].astype(F32)).astype(BF16)


def _attnprep(p, cos4, sin4, gains, ones_blk, *, tm, blk):
    m = p.shape[1]
    n_sub = tm // blk
    return pl.pallas_call(
        functools.partial(_attnprep_kernel, blk=blk, n_sub=n_sub),
        grid=(m // tm, HEADS),
        in_specs=[
            pl.BlockSpec((None, tm, LANES), lambda i, h: (SLAB_AQ + h, i, 0)),
            pl.BlockSpec((None, tm, LANES), lambda i, h: (SLAB_AK + h, i, 0)),
            pl.BlockSpec((None, tm, LANES), lambda i, h: (SLAB_AV + h, i, 0)),
            pl.BlockSpec((tm, LANES), lambda i, h: (i, 0)),
            pl.BlockSpec((tm, LANES), lambda i, h: (i, 0)),
            pl.BlockSpec((1, 2 * LANES), lambda i, h: (0, 0)),
            pl.BlockSpec((2 * LANES, 2 * LANES), lambda i, h: (0, 0)),
        ],
        out_specs=[
            pl.BlockSpec((None, n_sub, LANES, blk), lambda i, h: (h, i, 0, 0)),
            pl.BlockSpec((None, tm, LANES), lambda i, h: (h, i, 0)),
            pl.BlockSpec((None, n_sub, V_DIM, blk), lambda i, h: (h, i, 0, 0)),
        ],
        out_shape=[
            jax.ShapeDtypeStruct((HEADS, m // blk, LANES, blk), BF16),
            jax.ShapeDtypeStruct((HEADS, m, LANES), BF16),
            jax.ShapeDtypeStruct((HEADS, m // blk, V_DIM, blk), BF16),
        ],
        compiler_params=pltpu.CompilerParams(
            dimension_semantics=("arbitrary", "arbitrary"), vmem_limit_bytes=VMEM_LIMIT),
        name="attnprep",
    )(p, p, p, cos4, sin4, gains, ones_blk)


def _attn_kernel(q_ref, k_ref, vt_ref, az_ref, lam_ref, g_ref, o_ref,
                 qs_ref, sa_ref, sb_ref, ca_ref, cb_ref, m_ref, l_ref, acc_ref, *, blk, lam_init):
    nb = q_ref.shape[0]

    def produce(kc, s_ref, c_ref, masked):
        k = k_ref[pl.ds(pl.multiple_of(kc * blk, blk), blk), :]
        s = jnp.dot(k, qs_ref[...], preferred_element_type=F32)
        if masked:
            kpos = lax.broadcasted_iota(jnp.int32, s.shape, 0)
            qpos = lax.broadcasted_iota(jnp.int32, s.shape, 1) & (blk - 1)
            s = jnp.where(kpos <= qpos, s, -jnp.inf)
        s_ref[...] = s
        c_ref[...] = jnp.max(s, axis=0, keepdims=True)

    def consume(kc, s_ref, c_ref):
        m = m_ref[...]
        m_new = jnp.maximum(m, c_ref[...])
        alpha = jnp.exp2(m - m_new)
        p = jnp.exp2(s_ref[...] - m_new)
        l_ref[...] = alpha * l_ref[...] + jnp.sum(p, axis=0, keepdims=True)
        pv = jnp.dot(vt_ref[kc], p.astype(BF16), preferred_element_type=F32)
        acc_ref[...] = acc_ref[...] * alpha + pv
        m_ref[...] = m_new

    buf_a = (sa_ref, ca_ref)
    buf_b = (sb_ref, cb_ref)

    def two_chunks(j):
        produce(j + 1, *buf_b, False)
        consume(j, *buf_a)
        produce(j + 2, *buf_a, False)
        consume(j + 1, *buf_b)

    def pair(t, carry):
        two_chunks(2 * t)
        return carry

    def quad(t, carry):
        two_chunks(4 * t)
        two_chunks(4 * t + 2)
        return carry

    lq = lam_ref[...]
    lam = (jnp.exp(jnp.sum(lq[0:1] * lq[1:2], axis=-1, keepdims=True))
           - jnp.exp(jnp.sum(lq[2:3] * lq[3:4], axis=-1, keepdims=True)) + lam_init)

    def qblock(qi, carry):
        rows = pl.ds(pl.multiple_of(qi * blk, blk), blk)
        q_t = q_ref[qi]
        dim = lax.broadcasted_iota(jnp.int32, q_t.shape, 0)
        zero = jnp.zeros_like(q_t)
        map0 = (dim & ROPE_HALF) == 0
        qs_ref[:, 0:blk] = jnp.where(map0, q_t, zero)
        qs_ref[:, blk:2 * blk] = jnp.where(map0, zero, q_t)
        acc_ref[...] = jnp.zeros(acc_ref.shape, F32)
        m_ref[...] = jnp.full(m_ref.shape, -jnp.inf, F32)
        l_ref[...] = jnp.zeros(l_ref.shape, F32)

        @pl.when(qi == 0)
        def _():
            produce(0, *buf_a, True)
            consume(0, *buf_a)

        @pl.when(qi > 0)
        def _():
            produce(0, *buf_a, False)

        n_pairs = jnp.maximum(qi - 1, 0) // 2
        n_quads = n_pairs // 2
        lax.fori_loop(0, n_quads, quad, 0)
        lax.fori_loop(2 * n_quads, n_pairs, pair, 0)
        rest = qi - 2 * n_pairs

        @pl.when(rest == 1)
        def _():
            produce(qi, *buf_b, True)
            consume(qi - 1, *buf_a)
            consume(qi, *buf_b)

        @pl.when(rest == 2)
        def _():
            produce(qi - 1, *buf_b, False)
            consume(qi - 2, *buf_a)
            produce(qi, *buf_a, True)
            consume(qi - 1, *buf_b)
            consume(qi, *buf_a)

        o_t = acc_ref[...] / l_ref[...]
        o = jnp.transpose(o_t[:, 0:blk] - lam * o_t[:, blk:2 * blk])
        ms = jnp.mean(o * o, axis=-1, keepdims=True)
        y = (o * lax.rsqrt(ms + EPS)) * g_ref[...] * (1.0 - lam_init)
        az = az_ref[rows, :].astype(F32)
        o_ref[rows, :] = (y * (az * _sigmoid(az))).astype(BF16)
        return carry

    lax.fori_loop(0, nb, qblock, 0)


def _attention(q, k, vt, p, lambda_qk, attn_norm_g, *, bsz, seq, blk, lam_init):
    m = bsz * seq
    nb = seq // blk

    return pl.pallas_call(
        functools.partial(_attn_kernel, blk=blk, lam_init=lam_init),
        grid=(bsz * HEADS,),
        in_specs=[
            pl.BlockSpec((None, nb, LANES, blk), lambda bh: (bh % HEADS, bh // HEADS, 0, 0)),
            pl.BlockSpec((None, seq, LANES), lambda bh: (bh % HEADS, bh // HEADS, 0)),
            pl.BlockSpec((None, nb, V_DIM, blk), lambda bh: (bh % HEADS, bh // HEADS, 0, 0)),
            pl.BlockSpec((None, seq, LANES), lambda bh: (SLAB_AZ + bh % HEADS, bh // HEADS, 0)),
            pl.BlockSpec((4, QK_DIM), lambda bh: (0, 0)),
            pl.BlockSpec((1, V_DIM), lambda bh: (0, 0)),
        ],
        out_specs=pl.BlockSpec((None, seq, LANES), lambda bh: (bh % HEADS, bh // HEADS, 0)),
        out_shape=jax.ShapeDtypeStruct((HEADS, m, LANES), BF16),
        scratch_shapes=[
            pltpu.VMEM((LANES, 2 * blk), BF16),
            pltpu.VMEM((blk, 2 * blk), F32),
            pltpu.VMEM((blk, 2 * blk), F32),
            pltpu.VMEM((1, 2 * blk), F32),
            pltpu.VMEM((1, 2 * blk), F32),
            pltpu.VMEM((1, 2 * blk), F32),
            pltpu.VMEM((1, 2 * blk), F32),
            pltpu.VMEM((V_DIM, 2 * blk), F32),
        ],
        compiler_params=pltpu.CompilerParams(
            dimension_semantics=("arbitrary",), vmem_limit_bytes=VMEM_LIMIT),
        name="diffattn",
    )(q, k, vt, p, lambda_qk, attn_norm_g)


def _log_sigmoid(x):
    return jnp.minimum(x, 0.0) - jnp.log(1.0 + jnp.exp(-jnp.abs(x)))


def _mlstm_kernel(qk_ref, v_ref, bo_ref, bz_ref, gate_ref, gbias_ref, cw_ref, cb_ref, ng_ref,
                  oa_ref, ga_ref, gb_ref, x_ref, wa_ref, wb_ref, wo_ref,
                  o_ref, tail_ref, c_ref, n_ref, m_ref, *, chunk):
    L = chunk
    n_pair = HEADS // 2

    @pl.when(pl.program_id(1) == 0)
    def _():
        tail_ref[...] = jnp.zeros(tail_ref.shape, F32)
        c_ref[...] = jnp.zeros(c_ref.shape, F32)
        n_ref[...] = jnp.zeros(n_ref.shape, F32)
        m_ref[...] = jnp.full(m_ref.shape, NEG_INIT, F32)

    conv = []
    for sl in range(2 * n_pair):
        x = qk_ref[sl].astype(F32)
        tail_ref[sl, SUBLANES:SUBLANES + L, :] = x
        y = x * cw_ref[(CONV_WIDTH - 1) * 8 + sl: (CONV_WIDTH - 1) * 8 + sl + 1, :] + cb_ref[sl:sl + 1, :]
        for d in range(1, CONV_WIDTH):
            j = CONV_WIDTH - 1 - d
            y = y + tail_ref[sl, SUBLANES - d:SUBLANES - d + L, :] * cw_ref[j * 8 + sl: j * 8 + sl + 1, :]
        tail_ref[sl, 0:SUBLANES, :] = x[L - SUBLANES:L]
        conv.append(y * _sigmoid(y))
    lane = lax.broadcasted_iota(jnp.int32, (L, LANES), 1)
    lo_half = lane < QK_DIM

    g = gate_ref[...] + gbias_ref[...]
    ls = _log_sigmoid(g)
    r_i = lax.broadcasted_iota(jnp.int32, (L, L), 0)
    c_i = lax.broadcasted_iota(jnp.int32, (L, L), 1)
    tri = jnp.where(c_i <= r_i, 1.0, 0.0).astype(BF16)
    ls_hi, ls_lo = _split_hi_lo(ls)
    bcum = (jnp.dot(tri, ls_hi, preferred_element_type=F32)
            + jnp.dot(tri, ls_lo, preferred_element_type=F32))
    g_t = jnp.transpose(g)
    b_t = jnp.transpose(bcum)
    causal_t = r_i <= c_i
    lane1 = lax.broadcasted_iota(jnp.int32, (1, LANES), 1)
    nt = (((1,), (1,)), ((), ()))

    hb = [None] * HEADS
    for pr in range(n_pair):
        q_bf = (conv[pr] * (QK_DIM ** -0.5)).astype(BF16)
        k_bf = conv[n_pair + pr].astype(BF16)
        zero = jnp.zeros_like(q_bf)
        q_stack = jnp.concatenate([jnp.where(lo_half, q_bf, zero), jnp.where(lo_half, zero, q_bf)], axis=0)
        k_halves = (jnp.where(lo_half, k_bf, zero), jnp.where(lo_half, zero, k_bf))
        s_pair = lax.dot_general(k_bf, q_stack, nt, preferred_element_type=F32)
        ct_pair = c_ref[pr]
        inter_pair = lax.dot_general(ct_pair.astype(BF16), q_stack, nt, preferred_element_type=F32)
        n_pair_row = n_ref[pr]
        qn_pair = lax.dot_general(jnp.broadcast_to(n_pair_row, (2 * SUBLANES, LANES)).astype(BF16), q_stack, nt,
                                  preferred_element_type=F32)[0:1]

        a_vals, c_upd, n_upd = [], [], []
        for half in range(2):
            h = 2 * pr + half
            cols = slice(half * L, (half + 1) * L)
            ccol = g[:, h: h + 1] - bcum[:, HEADS + h: HEADS + h + 1]
            brow = b_t[HEADS + h: HEADS + h + 1, :]
            irow = g_t[h: h + 1, :]
            blast = brow[:, L - 1: L]
            m_prev = m_ref[h]

            d_t = jnp.where(causal_t, ccol + brow, -jnp.inf)
            inter_log = brow + m_prev
            m_t = jnp.maximum(inter_log, jnp.max(d_t, axis=0, keepdims=True))
            sc = s_pair[:, cols] * jnp.exp(d_t - m_t)
            inter_w = jnp.exp(inter_log - m_t)
            v_t = jnp.transpose(v_ref[h].astype(F32))
            num = (jnp.dot(v_t.astype(BF16), sc.astype(BF16), preferred_element_type=F32)
                   + inter_w * inter_pair[:, cols])
            den = jnp.sum(sc, axis=0, keepdims=True) + inter_w * qn_pair[:, cols]
            h_t = num / jnp.maximum(jnp.abs(den), jnp.exp(-m_t))

            ms = jnp.mean(h_t * h_t, axis=0, keepdims=True)
            y = jnp.transpose(h_t * lax.rsqrt(ms + EPS)) * ng_ref[...]
            bo = bo_ref[h].astype(F32)
            bz = bz_ref[h].astype(F32)
            hb[h] = (_sigmoid(bo) * y * (bz * _sigmoid(bz))).astype(BF16)

            w_log = blast - brow + irow
            m_new = jnp.maximum(blast + m_prev, jnp.max(w_log, axis=-1, keepdims=True))
            a_vals.append(jnp.exp(blast + m_prev - m_new))
            w_row = jnp.exp(w_log - m_new)
            c_upd.append(jnp.dot((v_t * w_row).astype(BF16), k_halves[half], preferred_element_type=F32))
            n_upd.append(jnp.dot(jnp.broadcast_to(w_row, (2 * SUBLANES, L)).astype(BF16), k_halves[half],
                                 preferred_element_type=F32)[0:1])
            m_ref[h] = m_new

        a_row = jnp.where(lane1 < QK_DIM, a_vals[0], a_vals[1])
        c_ref[pr] = a_row * ct_pair + c_upd[0] + c_upd[1]
        n_ref[pr] = a_row * n_pair_row + n_upd[0] + n_upd[1]

    def cat(ref):
        return jnp.concatenate([ref[h] for h in range(HEADS)], axis=1)

    ya = jnp.dot(cat(oa_ref), wa_ref[...], preferred_element_type=F32)
    yb = jnp.dot(jnp.concatenate(hb, axis=1), wb_ref[...], preferred_element_type=F32)
    u = _sigmoid(cat(ga_ref).astype(F32)) * ya + _sigmoid(cat(gb_ref).astype(F32)) * yb
    o_ref[...] = x_ref[...] + jnp.dot(u.astype(BF16), wo_ref[...], preferred_element_type=F32)


def _mlstm_out(p, gates, gate_bias, conv_w, conv_b, norm_g, oa, x2, wa, wb, wo, *, layer, bsz, seq, chunk):
    m = bsz * seq
    nc = seq // chunk

    def slab_map(group):
        return lambda b, c: (group, b * nc + c, 0)

    w_spec = pl.BlockSpec((None, D_MODEL, D_MODEL), lambda b, c: (layer, 0, 0))
    return pl.pallas_call(
        functools.partial(_mlstm_kernel, chunk=chunk),
        grid=(bsz, nc),
        in_specs=[
            pl.BlockSpec((HEADS, chunk, LANES), slab_map(SLAB_BQK // HEADS)),
            pl.BlockSpec((HEADS, chunk, LANES), slab_map(SLAB_BV // HEADS)),
            pl.BlockSpec((HEADS, chunk, LANES), slab_map(SLAB_BO // HEADS)),
            pl.BlockSpec((HEADS, chunk, LANES), slab_map(SLAB_BZ // HEADS)),
            pl.BlockSpec((chunk, LANES), lambda b, c: (b * nc + c, 0)),
            pl.BlockSpec((1, LANES), lambda b, c: (0, 0)),
            pl.BlockSpec((CONV_WIDTH * 8, LANES), lambda b, c: (0, 0)),
            pl.BlockSpec((8, LANES), lambda b, c: (0, 0)),
            pl.BlockSpec((1, V_DIM), lambda b, c: (0, 0)),
            pl.BlockSpec((HEADS, chunk, LANES), slab_map(0)),
            pl.BlockSpec((HEADS, chunk, LANES), slab_map(SLAB_GA // HEADS)),
            pl.BlockSpec((HEADS, chunk, LANES), slab_map(SLAB_GB // HEADS)),
            pl.BlockSpec((chunk, D_MODEL), lambda b, c: (b * nc + c, 0)),
            w_spec, w_spec, w_spec,
        ],
        out_specs=pl.BlockSpec((chunk, D_MODEL), lambda b, c: (b * nc + c, 0)),
        out_shape=jax.ShapeDtypeStruct((m, D_MODEL), F32),
        scratch_shapes=[
            pltpu.VMEM((HEADS, SUBLANES + chunk, LANES), F32),
            pltpu.VMEM((HEADS // 2, V_DIM, LANES), F32),
            pltpu.VMEM((HEADS // 2, 1, LANES), F32),
            pltpu.VMEM((HEADS, 1, 1), F32),
        ],
        compiler_params=pltpu.CompilerParams(
            dimension_semantics=("arbitrary", "arbitrary"), vmem_limit_bytes=VMEM_LIMIT),
        name="mlstm_out",
    )(p, p, p, p, gates, gate_bias, conv_w, conv_b, norm_g, oa, p, p, x2, wa, wb, wo)


def _reorder_w_in(w_in):
    idx = [0]
    for s in _REF_SPLITS:
        idx.append(idx[-1] + s)
    gate0, gate1 = idx[6], idx[8]
    n_qk = idx[2]
    w_qk = w_in[:, :, :n_qk].reshape(w_in.shape[:2] + (n_qk // LANES, 2, 2, ROPE_HALF))
    w_qk = jnp.swapaxes(w_qk, 3, 4).reshape(w_in.shape[:2] + (n_qk,))
    w_main = jnp.concatenate([w_qk, w_in[:, :, n_qk:gate0], w_in[:, :, gate1:]], axis=2).astype(BF16)
    pad = jnp.zeros(w_in.shape[:2] + (LANES - (gate1 - gate0),), w_in.dtype)
    w_gate = jnp.concatenate([w_in[:, :, gate0:gate1], pad], axis=2).astype(BF16)
    return w_main, w_gate


def _rope_lanes(per_dim):
    lo, hi = per_dim[:ROPE_HALF], per_dim[ROPE_HALF:]
    return jnp.concatenate([lo, lo, hi, hi])


def _pick(n, pref):
    return pref if n % pref == 0 else n


def kernel(x, positions, norm_g, w_in, q_norm_g, k_norm_g, lambda_qk, attn_norm_g, w_out_a,
           conv_w, conv_b, igate_b, fgate_b, mlstm_norm_g, w_out_b, w_o):
    bsz, seq, _ = x.shape
    m = bsz * seq
    depth = w_in.shape[0]
    tm = _pick(m, 1024)
    blk = _pick(seq, 512)
    chunk = _pick(seq, 256)

    inv_freq = ROPE_THETA ** (-jnp.arange(0, QK_DIM, 2, dtype=F32) / QK_DIM)
    ang = positions.astype(F32).reshape(m, 1) * inv_freq[None, :]
    cos, sin = jnp.cos(ang), jnp.sin(ang)
    cos4 = jnp.concatenate([cos, cos, cos, cos], axis=1)
    sin4 = jnp.concatenate([-sin, -sin, sin, sin], axis=1)
    lanes2 = jnp.arange(2 * LANES)
    grp = 2 * (lanes2 // LANES) + (lanes2 // ROPE_HALF) % 2
    ones_blk = (grp[:, None] == grp[None, :]).astype(BF16)

    x2 = x.reshape(m, D_MODEL)
    w_main, w_gate = _reorder_w_in(w_in)
    wa, wb, wo = w_out_a.astype(BF16), w_out_b.astype(BF16), w_o.astype(BF16)
    for l in range(depth):
        lam_init = 0.8 - 0.6 * math.exp(-0.3 * l)
        p, gates = _inproj(x2, norm_g[l][None, :], w_main, w_gate, layer=l, tm=tm, tn=2048)

        gq = _rope_lanes(q_norm_g[l]) * (QK_DIM ** -0.5 * LOG2_E)
        gk = _rope_lanes(k_norm_g[l])
        gains = jnp.concatenate([gq, gk])[None, :]
        q, k, vt = _attnprep(p, cos4, sin4, gains, ones_blk, tm=_pick(m, 2048), blk=blk)
        oa = _attention(q, k, vt, p, lambda_qk[l], attn_norm_g[l][None, :],
                        bsz=bsz, seq=seq, blk=blk, lam_init=lam_init)

        gate_bias = jnp.concatenate([igate_b[l], fgate_b[l], jnp.zeros((LANES - 2 * HEADS,), F32)])[None, :]
        cw = conv_w[l].reshape(CONV_WIDTH * 8, LANES)
        cb = conv_b[l].reshape(8, LANES)
        x2 = _mlstm_out(p, gates, gate_bias, cw, cb, mlstm_norm_g[l][None, :], oa, x2, wa, wb, wo,
                        layer=l, bsz=bsz, seq=seq, chunk=chunk)
    return x2.reshape(bsz, seq, D_MODEL)
```

```python
import functools
import math

import jax
import jax.numpy as jnp
from jax import lax
from jax.experimental import pallas as pl
from jax.experimental.pallas import tpu as pltpu

F32 = jnp.float32
BF16 = jnp.bfloat16

D_MODEL = 1024
HEADS = 8
QK_DIM = 64
V_DIM = 128
ROPE_HALF = QK_DIM // 2
CONV_WIDTH = 4
ROPE_THETA = 10000.0
EPS = 1e-6
NEG_INIT = -1e30
LOG2_E = math.log2(math.e)
LANES = 128
SUBLANES = 8
VMEM_LIMIT = 48 * 1024 * 1024

SLAB_AQ, SLAB_AK, SLAB_AV, SLAB_AZ = 0, 8, 16, 24
SLAB_BQK, SLAB_BV, SLAB_BO, SLAB_BZ, SLAB_GA, SLAB_GB = 32, 40, 48, 56, 64, 72
N_SLABS = 80
N_MAIN = N_SLABS * LANES

_REF_SPLITS = (1024, 1024, 1024, 1024, 1024, 1024, 8, 8, 1024, 1024, 1024, 1024)


def _split_hi_lo(x):
    hi = x.astype(BF16)
    lo = (x - hi.astype(F32)).astype(BF16)
    return hi, lo


def _sigmoid(x):
    return 1.0 / (1.0 + jnp.exp2(x * (-LOG2_E)))


def _inproj_kernel(x_ref, g_ref, w_ref, wg_ref, p_ref, gate_ref, h_ref, *, n_sub):
    @pl.when(pl.program_id(1) == 0)
    def _():
        x = x_ref[...]
        ms = jnp.mean(x * x, axis=-1, keepdims=True)
        h = ((x * lax.rsqrt(ms + EPS)) * g_ref[...]).astype(BF16)
        h_ref[...] = h
        gate_ref[...] = jnp.dot(h, wg_ref[...], preferred_element_type=F32)

    acc = jnp.dot(h_ref[...], w_ref[...], preferred_element_type=F32)
    for c in range(n_sub):
        p_ref[c] = acc[:, c * LANES:(c + 1) * LANES].astype(BF16)


def _inproj(x2, norm_g, w_main, w_gate, *, layer, tm, tn):
    m = x2.shape[0]
    n_sub = tn // LANES
    return pl.pallas_call(
        functools.partial(_inproj_kernel, n_sub=n_sub),
        grid=(m // tm, N_MAIN // tn),
        in_specs=[
            pl.BlockSpec((tm, D_MODEL), lambda i, j: (i, 0)),
            pl.BlockSpec((1, D_MODEL), lambda i, j: (0, 0)),
            pl.BlockSpec((None, D_MODEL, tn), lambda i, j: (layer, 0, j)),
            pl.BlockSpec((None, D_MODEL, LANES), lambda i, j: (layer, 0, 0)),
        ],
        out_specs=[
            pl.BlockSpec((n_sub, tm, LANES), lambda i, j: (j, i, 0)),
            pl.BlockSpec((tm, LANES), lambda i, j: (i, 0)),
        ],
        out_shape=[
            jax.ShapeDtypeStruct((N_SLABS, m, LANES), BF16),
            jax.ShapeDtypeStruct((m, LANES), F32),
        ],
        scratch_shapes=[pltpu.VMEM((tm, D_MODEL), BF16)],
        compiler_params=pltpu.CompilerParams(
            dimension_semantics=("arbitrary", "arbitrary"), vmem_limit_bytes=VMEM_LIMIT),
        name="inproj",
    )(x2, norm_g, w_main, w_gate)


def _attnprep_kernel(q_ref, k_ref, v_ref, cos_ref, sin_ref, g_ref, ones_ref, qt_ref, ko_ref, vt_ref,
                     *, blk, n_sub):
    x = jnp.concatenate([q_ref[...], k_ref[...]], axis=1).astype(F32)
    hi, lo = _split_hi_lo(x * x)
    ones = ones_ref[...]
    ss = (jnp.dot(hi, ones, preferred_element_type=F32)
          + jnp.dot(lo, ones, preferred_element_type=F32))
    y = (x * lax.rsqrt(ss * (1.0 / QK_DIM) + EPS)) * g_ref[...]
    cos = cos_ref[...]
    sin = sin_ref[...]
    roped = []
    for side in range(2):
        ys = y[:, side * LANES:(side + 1) * LANES]
        roped.append(ys * cos + pltpu.roll(ys, LANES // 2, axis=1) * sin)
    ko_ref[...] = roped[1].astype(BF16)
    for c in range(n_sub):
        rows = slice(c * blk, (c + 1) * blk)
        qt_ref[c] = jnp.transpose(roped[0][rows]).astype(BF16)
        vt_ref[c] = jnp.transpose(v_ref[rows, :].astype(F32)).astype(BF16)


def _attnprep(p, cos4, sin4, gains, ones_blk, *, tm, blk):
    m = p.shape[1]
    n_sub = tm // blk
    return pl.pallas_call(
        functools.partial(_attnprep_kernel, blk=blk, n_sub=n_sub),
        grid=(m // tm, HEADS),
        in_specs=[
            pl.BlockSpec((None, tm, LANES), lambda i, h: (SLAB_AQ + h, i, 0)),
            pl.BlockSpec((None, tm, LANES), lambda i, h: (SLAB_AK + h, i, 0)),
            pl.BlockSpec((None, tm, LANES), lambda i, h: (SLAB_AV + h, i, 0)),
            pl.BlockSpec((tm, LANES), lambda i, h: (i, 0)),
            pl.BlockSpec((tm, LANES), lambda i, h: (i, 0)),
            pl.BlockSpec((1, 2 * LANES), lambda i, h: (0, 0)),
            pl.BlockSpec((2 * LANES, 2 * LANES), lambda i, h: (0, 0)),
        ],
        out_specs=[
            pl.BlockSpec((None, n_sub, LANES, blk), lambda i, h: (h, i, 0, 0)),
            pl.BlockSpec((None, tm, LANES), lambda i, h: (h, i, 0)),
            pl.BlockSpec((None, n_sub, V_DIM, blk), lambda i, h: (h, i, 0, 0)),
        ],
        out_shape=[
            jax.ShapeDtypeStruct((HEADS, m // blk, LANES, blk), BF16),
            jax.ShapeDtypeStruct((HEADS, m, LANES), BF16),
            jax.ShapeDtypeStruct((HEADS, m // blk, V_DIM, blk), BF16),
        ],
        compiler_params=pltpu.CompilerParams(
            dimension_semantics=("arbitrary", "arbitrary"), vmem_limit_bytes=VMEM_LIMIT),
        name="attnprep",
    )(p, p, p, cos4, sin4, gains, ones_blk)


def _attn_kernel(q_ref, k_ref, vt_ref, az_ref, lam_ref, g_ref, o_ref,
                 qs_ref, sa_ref, sb_ref, ca_ref, cb_ref, m_ref, l_ref, acc_ref, *, blk, lam_init):
    nb = q_ref.shape[0]

    def produce(kc, s_ref, c_ref, masked):
        k = k_ref[pl.ds(pl.multiple_of(kc * blk, blk), blk), :]
        s = jnp.dot(k, qs_ref[...], preferred_element_type=F32)
        if masked:
            kpos = lax.broadcasted_iota(jnp.int32, s.shape, 0)
            qpos = lax.broadcasted_iota(jnp.int32, s.shape, 1) & (blk - 1)
            s = jnp.where(kpos <= qpos, s, -jnp.inf)
        s_ref[...] = s
        c_ref[...] = jnp.max(s, axis=0, keepdims=True)

    def consume(kc, s_ref, c_ref):
        m = m_ref[...]
        m_new = jnp.maximum(m, c_ref[...])
        alpha = jnp.exp2(m - m_new)
        p = jnp.exp2(s_ref[...] - m_new)
        l_ref[...] = alpha * l_ref[...] + jnp.sum(p, axis=0, keepdims=True)
        pv = jnp.dot(vt_ref[kc], p.astype(BF16), preferred_element_type=F32)
        acc_ref[...] = acc_ref[...] * alpha + pv
        m_ref[...] = m_new

    buf_a = (sa_ref, ca_ref)
    buf_b = (sb_ref, cb_ref)

    def two_chunks(j):
        produce(j + 1, *buf_b, False)
        consume(j, *buf_a)
        produce(j + 2, *buf_a, False)
        consume(j + 1, *buf_b)

    def pair(t, carry):
        two_chunks(2 * t)
        return carry

    def quad(t, carry):
        two_chunks(4 * t)
        two_chunks(4 * t + 2)
        return carry

    lq = lam_ref[...]
    lam = (jnp.exp(jnp.sum(lq[0:1] * lq[1:2], axis=-1, keepdims=True))
           - jnp.exp(jnp.sum(lq[2:3] * lq[3:4], axis=-1, keepdims=True)) + lam_init)

    def setup(qi):
        q_t = q_ref[qi]
        dim = lax.broadcasted_iota(jnp.int32, q_t.shape, 0)
        zero = jnp.zeros_like(q_t)
        map0 = (dim & ROPE_HALF) == 0
        qs_ref[:, 0:blk] = jnp.where(map0, q_t, zero)
        qs_ref[:, blk:2 * blk] = jnp.where(map0, zero, q_t)
        acc_ref[...] = jnp.zeros(acc_ref.shape, F32)
        m_ref[...] = jnp.full(m_ref.shape, -jnp.inf, F32)
        l_ref[...] = jnp.zeros(l_ref.shape, F32)

    def finalize(qi):
        rows = pl.ds(pl.multiple_of(qi * blk, blk), blk)
        o_t = acc_ref[...] / l_ref[...]
        o = jnp.transpose(o_t[:, 0:blk] - lam * o_t[:, blk:2 * blk])
        ms = jnp.mean(o * o, axis=-1, keepdims=True)
        y = (o * lax.rsqrt(ms + EPS)) * g_ref[...] * (1.0 - lam_init)
        az = az_ref[rows, :].astype(F32)
        o_ref[rows, :] = (y * (az * _sigmoid(az))).astype(BF16)

    setup(0)
    produce(0, *buf_a, True)
    consume(0, *buf_a)

    def qblock(qi, carry):
        finalize(qi - 1)
        setup(qi)
        produce(0, *buf_a, False)

        n_pairs = (qi - 1) // 2
        n_quads = n_pairs // 2
        lax.fori_loop(0, n_quads, quad, 0)
        lax.fori_loop(2 * n_quads, n_pairs, pair, 0)
        rest = qi - 2 * n_pairs

        @pl.when(rest == 1)
        def _():
            produce(qi, *buf_b, True)
            consume(qi - 1, *buf_a)
            consume(qi, *buf_b)

        @pl.when(rest == 2)
        def _():
            produce(qi - 1, *buf_b, False)
            consume(qi - 2, *buf_a)
            produce(qi, *buf_a, True)
            consume(qi - 1, *buf_b)
            consume(qi, *buf_a)

        return carry

    lax.fori_loop(1, nb, qblock, 0)
    finalize(nb - 1)


def _attention(q, k, vt, p, lambda_qk, attn_norm_g, *, bsz, seq, blk, lam_init):
    m = bsz * seq
    nb = seq // blk

    return pl.pallas_call(
        functools.partial(_attn_kernel, blk=blk, lam_init=lam_init),
        grid=(bsz * HEADS,),
        in_specs=[
            pl.BlockSpec((None, nb, LANES, blk), lambda bh: (bh % HEADS, bh // HEADS, 0, 0)),
            pl.BlockSpec((None, seq, LANES), lambda bh: (bh % HEADS, bh // HEADS, 0)),
            pl.BlockSpec((None, nb, V_DIM, blk), lambda bh: (bh % HEADS, bh // HEADS, 0, 0)),
            pl.BlockSpec((None, seq, LANES), lambda bh: (SLAB_AZ + bh % HEADS, bh // HEADS, 0)),
            pl.BlockSpec((4, QK_DIM), lambda bh: (0, 0)),
            pl.BlockSpec((1, V_DIM), lambda bh: (0, 0)),
        ],
        out_specs=pl.BlockSpec((None, seq, LANES), lambda bh: (bh % HEADS, bh // HEADS, 0)),
        out_shape=jax.ShapeDtypeStruct((HEADS, m, LANES), BF16),
        scratch_shapes=[
            pltpu.VMEM((LANES, 2 * blk), BF16),
            pltpu.VMEM((blk, 2 * blk), F32),
            pltpu.VMEM((blk, 2 * blk), F32),
            pltpu.VMEM((1, 2 * blk), F32),
            pltpu.VMEM((1, 2 * blk), F32),
            pltpu.VMEM((1, 2 * blk), F32),
            pltpu.VMEM((1, 2 * blk), F32),
            pltpu.VMEM((V_DIM, 2 * blk), F32),
        ],
        compiler_params=pltpu.CompilerParams(
            dimension_semantics=("arbitrary",), vmem_limit_bytes=VMEM_LIMIT),
        name="diffattn",
    )(q, k, vt, p, lambda_qk, attn_norm_g)


def _log_sigmoid(x):
    return jnp.minimum(x, 0.0) - jnp.log(1.0 + jnp.exp(-jnp.abs(x)))


def _mlstm_kernel(qk_ref, v_ref, bo_ref, bz_ref, gate_ref, gbias_ref, cw_ref, cb_ref, ng_ref,
                  oa_ref, ga_ref, gb_ref, x_ref, wa_ref, wb_ref, wo_ref,
                  o_ref, tail_ref, c_ref, n_ref, m_ref, *, chunk):
    L = chunk
    n_pair = HEADS // 2

    @pl.when(pl.program_id(1) == 0)
    def _():
        tail_ref[...] = jnp.zeros(tail_ref.shape, F32)
        c_ref[...] = jnp.zeros(c_ref.shape, F32)
        n_ref[...] = jnp.zeros(n_ref.shape, F32)
        m_ref[...] = jnp.full(m_ref.shape, NEG_INIT, F32)

    conv = []
    for sl in range(2 * n_pair):
        x = qk_ref[sl].astype(F32)
        tail_ref[sl, SUBLANES:SUBLANES + L, :] = x
        y = x * cw_ref[(CONV_WIDTH - 1) * 8 + sl: (CONV_WIDTH - 1) * 8 + sl + 1, :] + cb_ref[sl:sl + 1, :]
        for d in range(1, CONV_WIDTH):
            j = CONV_WIDTH - 1 - d
            y = y + tail_ref[sl, SUBLANES - d:SUBLANES - d + L, :] * cw_ref[j * 8 + sl: j * 8 + sl + 1, :]
        tail_ref[sl, 0:SUBLANES, :] = x[L - SUBLANES:L]
        conv.append(y * _sigmoid(y))
    lane = lax.broadcasted_iota(jnp.int32, (L, LANES), 1)
    lo_half = lane < QK_DIM

    g = gate_ref[...] + gbias_ref[...]
    ls = _log_sigmoid(g)
    r_i = lax.broadcasted_iota(jnp.int32, (L, L), 0)
    c_i = lax.broadcasted_iota(jnp.int32, (L, L), 1)
    tri = jnp.where(c_i <= r_i, 1.0, 0.0).astype(BF16)
    ls_hi, ls_lo = _split_hi_lo(ls)
    bcum = (jnp.dot(tri, ls_hi, preferred_element_type=F32)
            + jnp.dot(tri, ls_lo, preferred_element_type=F32))
    g_t = jnp.transpose(g)
    b_t = jnp.transpose(bcum)
    causal_t = r_i <= c_i
    lane1 = lax.broadcasted_iota(jnp.int32, (1, LANES), 1)
    nt = (((1,), (1,)), ((), ()))

    hb = [None] * HEADS
    for pr in range(n_pair):
        q_bf = (conv[pr] * (QK_DIM ** -0.5)).astype(BF16)
        k_bf = conv[n_pair + pr].astype(BF16)
        zero = jnp.zeros_like(q_bf)
        q_stack = jnp.concatenate([jnp.where(lo_half, q_bf, zero), jnp.where(lo_half, zero, q_bf)], axis=0)
        k_halves = (jnp.where(lo_half, k_bf, zero), jnp.where(lo_half, zero, k_bf))
        s_pair = lax.dot_general(k_bf, q_stack, nt, preferred_element_type=F32)
        ct_pair = c_ref[pr]
        inter_pair = lax.dot_general(ct_pair.astype(BF16), q_stack, nt, preferred_element_type=F32)
        n_pair_row = n_ref[pr]
        qn_pair = lax.dot_general(jnp.broadcast_to(n_pair_row, (2 * SUBLANES, LANES)).astype(BF16), q_stack, nt,
                                  preferred_element_type=F32)[0:1]

        a_vals, c_upd, n_upd = [], [], []
        for half in range(2):
            h = 2 * pr + half
            cols = slice(half * L, (half + 1) * L)
            ccol = g[:, h: h + 1] - bcum[:, HEADS + h: HEADS + h + 1]
            brow = b_t[HEADS + h: HEADS + h + 1, :]
            irow = g_t[h: h + 1, :]
            blast = brow[:, L - 1: L]
            m_prev = m_ref[h]

            d_t = jnp.where(causal_t, ccol + brow, -jnp.inf)
            inter_log = brow + m_prev
            m_t = jnp.maximum(inter_log, jnp.max(d_t, axis=0, keepdims=True))
            sc = s_pair[:, cols] * jnp.exp(d_t - m_t)
            inter_w = jnp.exp(inter_log - m_t)
            v_t = jnp.transpose(v_ref[h].astype(F32))
            num = (jnp.dot(v_t.astype(BF16), sc.astype(BF16), preferred_element_type=F32)
                   + inter_w * inter_pair[:, cols])
            den = jnp.sum(sc, axis=0, keepdims=True) + inter_w * qn_pair[:, cols]
            h_t = num / jnp.maximum(jnp.abs(den), jnp.exp(-m_t))

            ms = jnp.mean(h_t * h_t, axis=0, keepdims=True)
            y = jnp.transpose(h_t * lax.rsqrt(ms + EPS)) * ng_ref[...]
            bo = bo_ref[h].astype(F32)
            bz = bz_ref[h].astype(F32)
            hb[h] = (_sigmoid(bo) * y * (bz * _sigmoid(bz))).astype(BF16)

            w_log = blast - brow + irow
            m_new = jnp.maximum(blast + m_prev, jnp.max(w_log, axis=-1, keepdims=True))
            a_vals.append(jnp.exp(blast + m_prev - m_new))
            w_row = jnp.exp(w_log - m_new)
            c_upd.append(jnp.dot((v_t * w_row).astype(BF16), k_halves[half], preferred_element_type=F32))
            n_upd.append(jnp.dot(jnp.broadcast_to(w_row, (2 * SUBLANES, L)).astype(BF16), k_halves[half],
                                 preferred_element_type=F32)[0:1])
            m_ref[h] = m_new

        a_row = jnp.where(lane1 < QK_DIM, a_vals[0], a_vals[1])
        c_ref[pr] = a_row * ct_pair + c_upd[0] + c_upd[1]
        n_ref[pr] = a_row * n_pair_row + n_upd[0] + n_upd[1]

    def cat(ref):
        return jnp.concatenate([ref[h] for h in range(HEADS)], axis=1)

    ya = jnp.dot(cat(oa_ref), wa_ref[...], preferred_element_type=F32)
    yb = jnp.dot(jnp.concatenate(hb, axis=1), wb_ref[...], preferred_element_type=F32)
    u = _sigmoid(cat(ga_ref).astype(F32)) * ya + _sigmoid(cat(gb_ref).astype(F32)) * yb
    o_ref[...] = x_ref[...] + jnp.dot(u.astype(BF16), wo_ref[...], preferred_element_type=F32)


def _mlstm_out(p, gates, gate_bias, conv_w, conv_b, norm_g, oa, x2, wa, wb, wo, *, layer, bsz, seq, chunk):
    m = bsz * seq
    nc = seq // chunk

    def slab_map(group):
        return lambda b, c: (group, b * nc + c, 0)

    w_spec = pl.BlockSpec((None, D_MODEL, D_MODEL), lambda b, c: (layer, 0, 0))
    return pl.pallas_call(
        functools.partial(_mlstm_kernel, chunk=chunk),
        grid=(bsz, nc),
        in_specs=[
            pl.BlockSpec((HEADS, chunk, LANES), slab_map(SLAB_BQK // HEADS)),
            pl.BlockSpec((HEADS, chunk, LANES), slab_map(SLAB_BV // HEADS)),
            pl.BlockSpec((HEADS, chunk, LANES), slab_map(SLAB_BO // HEADS)),
            pl.BlockSpec((HEADS, chunk, LANES), slab_map(SLAB_BZ // HEADS)),
            pl.BlockSpec((chunk, LANES), lambda b, c: (b * nc + c, 0)),
            pl.BlockSpec((1, LANES), lambda b, c: (0, 0)),
            pl.BlockSpec((CONV_WIDTH * 8, LANES), lambda b, c: (0, 0)),
            pl.BlockSpec((8, LANES), lambda b, c: (0, 0)),
            pl.BlockSpec((1, V_DIM), lambda b, c: (0, 0)),
            pl.BlockSpec((HEADS, chunk, LANES), slab_map(0)),
            pl.BlockSpec((HEADS, chunk, LANES), slab_map(SLAB_GA // HEADS)),
            pl.BlockSpec((HEADS, chunk, LANES), slab_map(SLAB_GB // HEADS)),
            pl.BlockSpec((chunk, D_MODEL), lambda b, c: (b * nc + c, 0)),
            w_spec, w_spec, w_spec,
        ],
        out_specs=pl.BlockSpec((chunk, D_MODEL), lambda b, c: (b * nc + c, 0)),
        out_shape=jax.ShapeDtypeStruct((m, D_MODEL), F32),
        scratch_shapes=[
            pltpu.VMEM((HEADS, SUBLANES + chunk, LANES), F32),
            pltpu.VMEM((HEADS // 2, V_DIM, LANES), F32),
            pltpu.VMEM((HEADS // 2, 1, LANES), F32),
            pltpu.VMEM((HEADS, 1, 1), F32),
        ],
        compiler_params=pltpu.CompilerParams(
            dimension_semantics=("arbitrary", "arbitrary"), vmem_limit_bytes=VMEM_LIMIT),
        name="mlstm_out",
    )(p, p, p, p, gates, gate_bias, conv_w, conv_b, norm_g, oa, p, p, x2, wa, wb, wo)


def _reorder_w_in(w_in):
    idx = [0]
    for s in _REF_SPLITS:
        idx.append(idx[-1] + s)
    gate0, gate1 = idx[6], idx[8]
    n_qk = idx[2]
    w_qk = w_in[:, :, :n_qk].reshape(w_in.shape[:2] + (n_qk // LANES, 2, 2, ROPE_HALF))
    w_qk = jnp.swapaxes(w_qk, 3, 4).reshape(w_in.shape[:2] + (n_qk,))
    w_main = jnp.concatenate([w_qk, w_in[:, :, n_qk:gate0], w_in[:, :, gate1:]], axis=2).astype(BF16)
    pad = jnp.zeros(w_in.shape[:2] + (LANES - (gate1 - gate0),), w_in.dtype)
    w_gate = jnp.concatenate([w_in[:, :, gate0:gate1], pad], axis=2).astype(BF16)
    return w_main, w_gate


def _rope_lanes(per_dim):
    lo, hi = per_dim[:ROPE_HALF], per_dim[ROPE_HALF:]
    return jnp.concatenate([lo, lo, hi, hi])


def _pick(n, pref):
    return pref if n % pref == 0 else n


def kernel(x, positions, norm_g, w_in, q_norm_g, k_norm_g, lambda_qk, attn_norm_g, w_out_a,
           conv_w, conv_b, igate_b, fgate_b, mlstm_norm_g, w_out_b, w_o):
    bsz, seq, _ = x.shape
    m = bsz * seq
    depth = w_in.shape[0]
    tm = _pick(m, 1024)
    blk = _pick(seq, 512)
    chunk = _pick(seq, 256)

    inv_freq = ROPE_THETA ** (-jnp.arange(0, QK_DIM, 2, dtype=F32) / QK_DIM)
    ang = positions.astype(F32).reshape(m, 1) * inv_freq[None, :]
    cos, sin = jnp.cos(ang), jnp.sin(ang)
    cos4 = jnp.concatenate([cos, cos, cos, cos], axis=1)
    sin4 = jnp.concatenate([-sin, -sin, sin, sin], axis=1)
    lanes2 = jnp.arange(2 * LANES)
    grp = 2 * (lanes2 // LANES) + (lanes2 // ROPE_HALF) % 2
    ones_blk = (grp[:, None] == grp[None, :]).astype(BF16)

    x2 = x.reshape(m, D_MODEL)
    w_main, w_gate = _reorder_w_in(w_in)
    wa, wb, wo = w_out_a.astype(BF16), w_out_b.astype(BF16), w_o.astype(BF16)
    for l in range(depth):
        lam_init = 0.8 - 0.6 * math.exp(-0.3 * l)
        p, gates = _inproj(x2, norm_g[l][None, :], w_main, w_gate, layer=l, tm=tm, tn=2560)

        gq = _rope_lanes(q_norm_g[l]) * (QK_DIM ** -0.5 * LOG2_E)
        gk = _rope_lanes(k_norm_g[l])
        gains = jnp.concatenate([gq, gk])[None, :]
        q, k, vt = _attnprep(p, cos4, sin4, gains, ones_blk, tm=_pick(m, 2048), blk=blk)
        oa = _attention(q, k, vt, p, lambda_qk[l], attn_norm_g[l][None, :],
                        bsz=bsz, seq=seq, blk=blk, lam_init=lam_init)

        gate_bias = jnp.concatenate([igate_b[l], fgate_b[l], jnp.zeros((LANES - 2 * HEADS,), F32)])[None, :]
        cw = conv_w[l].reshape(CONV_WIDTH * 8, LANES)
        cb = conv_b[l].reshape(8, LANES)
        x2 = _mlstm_out(p, gates, gate_bias, cw, cb, mlstm_norm_g[l][None, :], oa, x2, wa, wb, wo,
                        layer=l, bsz=bsz, seq=seq, chunk=chunk)
    return x2.reshape(bsz, seq, D_MODEL)
```

```python
import functools
import math

import jax
import jax.numpy as jnp
from jax import lax
from jax.experimental import pallas as pl
from jax.experimental.pallas import tpu as pltpu

F32 = jnp.float32
BF16 = jnp.bfloat16

D_MODEL = 1024
HEADS = 8
QK_DIM = 64
V_DIM = 128
ROPE_HALF = QK_DIM // 2
CONV_WIDTH = 4
ROPE_THETA = 10000.0
EPS = 1e-6
NEG_INIT = -1e30
LOG2_E = math.log2(math.e)
LANES = 128
SUBLANES = 8
VMEM_LIMIT = 48 * 1024 * 1024

SLAB_AQ, SLAB_AK, SLAB_AV, SLAB_AZ = 0, 8, 16, 24
SLAB_BQK, SLAB_BV, SLAB_BO, SLAB_BZ, SLAB_GA, SLAB_GB = 32, 40, 48, 56, 64, 72
N_SLABS = 80
N_MAIN = N_SLABS * LANES

_REF_SPLITS = (1024, 1024, 1024, 1024, 1024, 1024, 8, 8, 1024, 1024, 1024, 1024)


def _split_hi_lo(x):
    hi = x.astype(BF16)
    lo = (x - hi.astype(F32)).astype(BF16)
    return hi, lo


def _sigmoid(x):
    return 1.0 / (1.0 + jnp.exp2(x * (-LOG2_E)))


def _inproj_kernel(x_ref, g_ref, w_ref, wg_ref, p_ref, gate_ref, h_ref, *, n_sub):
    @pl.when(pl.program_id(1) == 0)
    def _():
        x = x_ref[...]
        ms = jnp.mean(x * x, axis=-1, keepdims=True)
        h = ((x * lax.rsqrt(ms + EPS)) * g_ref[...]).astype(BF16)
        h_ref[...] = h
        gate_ref[...] = jnp.dot(h, wg_ref[...], preferred_element_type=F32)

    acc = jnp.dot(h_ref[...], w_ref[...], preferred_element_type=F32)
    for c in range(n_sub):
        p_ref[c] = acc[:, c * LANES:(c + 1) * LANES].astype(BF16)


def _inproj(x2, norm_g, w_main, w_gate, *, layer, tm, tn):
    m = x2.shape[0]
    n_sub = tn // LANES
    return pl.pallas_call(
        functools.partial(_inproj_kernel, n_sub=n_sub),
        grid=(m // tm, N_MAIN // tn),
        in_specs=[
            pl.BlockSpec((tm, D_MODEL), lambda i, j: (i, 0)),
            pl.BlockSpec((1, D_MODEL), lambda i, j: (0, 0)),
            pl.BlockSpec((None, D_MODEL, tn), lambda i, j: (layer, 0, j)),
            pl.BlockSpec((None, D_MODEL, LANES), lambda i, j: (layer, 0, 0)),
        ],
        out_specs=[
            pl.BlockSpec((n_sub, tm, LANES), lambda i, j: (j, i, 0)),
            pl.BlockSpec((tm, LANES), lambda i, j: (i, 0)),
        ],
        out_shape=[
            jax.ShapeDtypeStruct((N_SLABS, m, LANES), BF16),
            jax.ShapeDtypeStruct((m, LANES), F32),
        ],
        scratch_shapes=[pltpu.VMEM((tm, D_MODEL), BF16)],
        compiler_params=pltpu.CompilerParams(
            dimension_semantics=("arbitrary", "arbitrary"), vmem_limit_bytes=VMEM_LIMIT),
        name="inproj",
    )(x2, norm_g, w_main, w_gate)


def _attnprep_kernel(q_ref, k_ref, v_ref, cos_ref, sin_ref, g_ref, ones_ref, qt_ref, ko_ref, vt_ref,
                     *, blk, n_sub):
    x = jnp.concatenate([q_ref[...], k_ref[...]], axis=1).astype(F32)
    hi, lo = _split_hi_lo(x * x)
    ones = ones_ref[...]
    ss = (jnp.dot(hi, ones, preferred_element_type=F32)
          + jnp.dot(lo, ones, preferred_element_type=F32))
    y = (x * lax.rsqrt(ss * (1.0 / QK_DIM) + EPS)) * g_ref[...]
    cos = cos_ref[...]
    sin = sin_ref[...]
    roped = []
    for side in range(2):
        ys = y[:, side * LANES:(side + 1) * LANES]
        roped.append(ys * cos + pltpu.roll(ys, LANES // 2, axis=1) * sin)
    ko_ref[...] = roped[1].astype(BF16)
    for c in range(n_sub):
        rows = slice(c * blk, (c + 1) * blk)
        qt_ref[c] = jnp.transpose(roped[0][rows]).astype(BF16)
        vt_ref[c] = jnp.transpose(v_ref[rows, :].astype(F32)).astype(BF16)


def _attnprep(p, cos4, sin4, gains, ones_blk, *, tm, blk):
    m = p.shape[1]
    n_sub = tm // blk
    return pl.pallas_call(
        functools.partial(_attnprep_kernel, blk=blk, n_sub=n_sub),
        grid=(m // tm, HEADS),
        in_specs=[
            pl.BlockSpec((None, tm, LANES), lambda i, h: (SLAB_AQ + h, i, 0)),
            pl.BlockSpec((None, tm, LANES), lambda i, h: (SLAB_AK + h, i, 0)),
            pl.BlockSpec((None, tm, LANES), lambda i, h: (SLAB_AV + h, i, 0)),
            pl.BlockSpec((tm, LANES), lambda i, h: (i, 0)),
            pl.BlockSpec((tm, LANES), lambda i, h: (i, 0)),
            pl.BlockSpec((1, 2 * LANES), lambda i, h: (0, 0)),
            pl.BlockSpec((2 * LANES, 2 * LANES), lambda i, h: (0, 0)),
        ],
        out_specs=[
            pl.BlockSpec((None, n_sub, LANES, blk), lambda i, h: (h, i, 0, 0)),
            pl.BlockSpec((None, tm, LANES), lambda i, h: (h, i, 0)),
            pl.BlockSpec((None, n_sub, V_DIM, blk), lambda i, h: (h, i, 0, 0)),
        ],
        out_shape=[
            jax.ShapeDtypeStruct((HEADS, m // blk, LANES, blk), BF16),
            jax.ShapeDtypeStruct((HEADS, m, LANES), BF16),
            jax.ShapeDtypeStruct((HEADS, m // blk, V_DIM, blk), BF16),
        ],
        compiler_params=pltpu.CompilerParams(
            dimension_semantics=("arbitrary", "arbitrary"), vmem_limit_bytes=VMEM_LIMIT),
        name="attnprep",
    )(p, p, p, cos4, sin4, gains, ones_blk)


def _attn_kernel(q_ref, k_ref, vt_ref, az_ref, lam_ref, g_ref, o_ref,
                 qs_ref, sa_ref, sb_ref, ca_ref, cb_ref, m_ref, l_ref, acc_ref, *, blk, lam_init):
    nb = q_ref.shape[0]

    def produce(kc, s_ref, c_ref, masked):
        k = k_ref[pl.ds(pl.multiple_of(kc * blk, blk), blk), :]
        s = jnp.dot(k, qs_ref[...], preferred_element_type=F32)
        if masked:
            kpos = lax.broadcasted_iota(jnp.int32, s.shape, 0)
            qpos = lax.broadcasted_iota(jnp.int32, s.shape, 1) & (blk - 1)
            s = jnp.where(kpos <= qpos, s, -jnp.inf)
        s_ref[...] = s
        c_ref[...] = jnp.max(s, axis=0, keepdims=True)

    def consume(kc, s_ref, c_ref):
        m = m_ref[...]
        m_new = jnp.maximum(m, c_ref[...])
        alpha = jnp.exp2(m - m_new)
        p = jnp.exp2(s_ref[...] - m_new)
        l_ref[...] = alpha * l_ref[...] + jnp.sum(p, axis=0, keepdims=True)
        pv = jnp.dot(vt_ref[kc], p.astype(BF16), preferred_element_type=F32)
        acc_ref[...] = acc_ref[...] * alpha + pv
        m_ref[...] = m_new

    buf_a = (sa_ref, ca_ref)
    buf_b = (sb_ref, cb_ref)

    def two_chunks(j):
        produce(j + 1, *buf_b, False)
        consume(j, *buf_a)
        produce(j + 2, *buf_a, False)
        consume(j + 1, *buf_b)

    def pair(t, carry):
        two_chunks(2 * t)
        return carry

    def quad(t, carry):
        two_chunks(4 * t)
        two_chunks(4 * t + 2)
        return carry

    lq = lam_ref[...]
    lam = (jnp.exp(jnp.sum(lq[0:1] * lq[1:2], axis=-1, keepdims=True))
           - jnp.exp(jnp.sum(lq[2:3] * lq[3:4], axis=-1, keepdims=True)) + lam_init)

    def setup(qi):
        q_t = q_ref[qi]
        dim = lax.broadcasted_iota(jnp.int32, q_t.shape, 0)
        zero = jnp.zeros_like(q_t)
        map0 = (dim & ROPE_HALF) == 0
        qs_ref[:, 0:blk] = jnp.where(map0, q_t, zero)
        qs_ref[:, blk:2 * blk] = jnp.where(map0, zero, q_t)
        acc_ref[...] = jnp.zeros(acc_ref.shape, F32)
        m_ref[...] = jnp.full(m_ref.shape, -jnp.inf, F32)
        l_ref[...] = jnp.zeros(l_ref.shape, F32)

    def finalize(qi):
        rows = pl.ds(pl.multiple_of(qi * blk, blk), blk)
        o_t = acc_ref[...] / l_ref[...]
        o = jnp.transpose(o_t[:, 0:blk] - lam * o_t[:, blk:2 * blk])
        ms = jnp.mean(o * o, axis=-1, keepdims=True)
        y = (o * lax.rsqrt(ms + EPS)) * g_ref[...] * (1.0 - lam_init)
        az = az_ref[rows, :].astype(F32)
        o_ref[rows, :] = (y * (az * _sigmoid(az))).astype(BF16)

    setup(0)
    produce(0, *buf_a, True)
    consume(0, *buf_a)

    def qblock(qi, carry):
        finalize(qi - 1)
        setup(qi)
        produce(0, *buf_a, False)

        n_pairs = (qi - 1) // 2
        n_quads = n_pairs // 2
        lax.fori_loop(0, n_quads, quad, 0)
        lax.fori_loop(2 * n_quads, n_pairs, pair, 0)
        rest = qi - 2 * n_pairs

        @pl.when(rest == 1)
        def _():
            produce(qi, *buf_b, True)
            consume(qi - 1, *buf_a)
            consume(qi, *buf_b)

        @pl.when(rest == 2)
        def _():
            produce(qi - 1, *buf_b, False)
            consume(qi - 2, *buf_a)
            produce(qi, *buf_a, True)
            consume(qi - 1, *buf_b)
            consume(qi, *buf_a)

        return carry

    lax.fori_loop(1, nb, qblock, 0)
    finalize(nb - 1)


def _attention(q, k, vt, p, lambda_qk, attn_norm_g, *, bsz, seq, blk, lam_init):
    m = bsz * seq
    nb = seq // blk

    return pl.pallas_call(
        functools.partial(_attn_kernel, blk=blk, lam_init=lam_init),
        grid=(bsz * HEADS,),
        in_specs=[
            pl.BlockSpec((None, nb, LANES, blk), lambda bh: (bh % HEADS, bh // HEADS, 0, 0)),
            pl.BlockSpec((None, seq, LANES), lambda bh: (bh % HEADS, bh // HEADS, 0)),
            pl.BlockSpec((None, nb, V_DIM, blk), lambda bh: (bh % HEADS, bh // HEADS, 0, 0)),
            pl.BlockSpec((None, seq, LANES), lambda bh: (SLAB_AZ + bh % HEADS, bh // HEADS, 0)),
            pl.BlockSpec((4, QK_DIM), lambda bh: (0, 0)),
            pl.BlockSpec((1, V_DIM), lambda bh: (0, 0)),
        ],
        out_specs=pl.BlockSpec((None, seq, LANES), lambda bh: (bh % HEADS, bh // HEADS, 0)),
        out_shape=jax.ShapeDtypeStruct((HEADS, m, LANES), BF16),
        scratch_shapes=[
            pltpu.VMEM((LANES, 2 * blk), BF16),
            pltpu.VMEM((blk, 2 * blk), F32),
            pltpu.VMEM((blk, 2 * blk), F32),
            pltpu.VMEM((1, 2 * blk), F32),
            pltpu.VMEM((1, 2 * blk), F32),
            pltpu.VMEM((1, 2 * blk), F32),
            pltpu.VMEM((1, 2 * blk), F32),
            pltpu.VMEM((V_DIM, 2 * blk), F32),
        ],
        compiler_params=pltpu.CompilerParams(
            dimension_semantics=("arbitrary",), vmem_limit_bytes=VMEM_LIMIT),
        name="diffattn",
    )(q, k, vt, p, lambda_qk, attn_norm_g)


def _log_sigmoid(x):
    return jnp.minimum(x, 0.0) - jnp.log(1.0 + jnp.exp(-jnp.abs(x)))


def _mlstm_kernel(qk_ref, v_ref, bo_ref, bz_ref, gate_ref, gbias_ref, cw_ref, cb_ref, ng_ref,
                  oa_ref, ga_ref, gb_ref, x_ref, wa_ref, wb_ref, wo_ref,
                  o_ref, tail_ref, c_ref, n_ref, m_ref, *, chunk):
    L = chunk
    n_pair = HEADS // 2

    @pl.when(pl.program_id(1) == 0)
    def _():
        tail_ref[...] = jnp.zeros(tail_ref.shape, F32)
        c_ref[...] = jnp.zeros(c_ref.shape, F32)
        n_ref[...] = jnp.zeros(n_ref.shape, F32)
        m_ref[...] = jnp.full(m_ref.shape, NEG_INIT, F32)

    conv = []
    for sl in range(2 * n_pair):
        x = qk_ref[sl].astype(F32)
        tail_ref[sl, SUBLANES:SUBLANES + L, :] = x
        y = x * cw_ref[(CONV_WIDTH - 1) * 8 + sl: (CONV_WIDTH - 1) * 8 + sl + 1, :] + cb_ref[sl:sl + 1, :]
        for d in range(1, CONV_WIDTH):
            j = CONV_WIDTH - 1 - d
            y = y + tail_ref[sl, SUBLANES - d:SUBLANES - d + L, :] * cw_ref[j * 8 + sl: j * 8 + sl + 1, :]
        tail_ref[sl, 0:SUBLANES, :] = x[L - SUBLANES:L]
        conv.append(y * _sigmoid(y))
    lane = lax.broadcasted_iota(jnp.int32, (L, LANES), 1)
    lo_half = lane < QK_DIM

    g = gate_ref[...] + gbias_ref[...]
    ls = _log_sigmoid(g)
    r_i = lax.broadcasted_iota(jnp.int32, (L, L), 0)
    c_i = lax.broadcasted_iota(jnp.int32, (L, L), 1)
    tri = jnp.where(c_i <= r_i, 1.0, 0.0).astype(BF16)
    ls_hi, ls_lo = _split_hi_lo(ls)
    bcum = (jnp.dot(tri, ls_hi, preferred_element_type=F32)
            + jnp.dot(tri, ls_lo, preferred_element_type=F32))
    g_t = jnp.transpose(g)
    b_t = jnp.transpose(bcum)
    causal_t = r_i <= c_i
    lane1 = lax.broadcasted_iota(jnp.int32, (1, LANES), 1)
    nt = (((1,), (1,)), ((), ()))

    hb = [None] * HEADS
    for pr in range(n_pair):
        q_bf = (conv[pr] * (QK_DIM ** -0.5)).astype(BF16)
        k_bf = conv[n_pair + pr].astype(BF16)
        zero = jnp.zeros_like(q_bf)
        q_stack = jnp.concatenate([jnp.where(lo_half, q_bf, zero), jnp.where(lo_half, zero, q_bf)], axis=0)
        k_halves = (jnp.where(lo_half, k_bf, zero), jnp.where(lo_half, zero, k_bf))
        s_pair = lax.dot_general(k_bf, q_stack, nt, preferred_element_type=F32)
        ct_pair = c_ref[pr]
        inter_pair = lax.dot_general(ct_pair.astype(BF16), q_stack, nt, preferred_element_type=F32)
        n_pair_row = n_ref[pr]
        qn_pair = lax.dot_general(jnp.broadcast_to(n_pair_row, (2 * SUBLANES, LANES)).astype(BF16), q_stack, nt,
                                  preferred_element_type=F32)[0:1]

        a_vals, c_upd, n_upd = [], [], []
        for half in range(2):
            h = 2 * pr + half
            cols = slice(half * L, (half + 1) * L)
            ccol = g[:, h: h + 1] - bcum[:, HEADS + h: HEADS + h + 1]
            brow = b_t[HEADS + h: HEADS + h + 1, :]
            irow = g_t[h: h + 1, :]
            blast = brow[:, L - 1: L]
            m_prev = m_ref[h]

            d_t = jnp.where(causal_t, ccol + brow, -jnp.inf)
            inter_log = brow + m_prev
            m_t = jnp.maximum(inter_log, jnp.max(d_t, axis=0, keepdims=True))
            sc = s_pair[:, cols] * jnp.exp(d_t - m_t)
            inter_w = jnp.exp(inter_log - m_t)
            v_t = jnp.transpose(v_ref[h].astype(F32))
            num = (jnp.dot(v_t.astype(BF16), sc.astype(BF16), preferred_element_type=F32)
                   + inter_w * inter_pair[:, cols])
            den = jnp.sum(sc, axis=0, keepdims=True) + inter_w * qn_pair[:, cols]
            h_t = num / jnp.maximum(jnp.abs(den), jnp.exp(-m_t))

            ms = jnp.mean(h_t * h_t, axis=0, keepdims=True)
            y = jnp.transpose(h_t * lax.rsqrt(ms + EPS)) * ng_ref[...]
            bo = bo_ref[h].astype(F32)
            bz = bz_ref[h].astype(F32)
            hb[h] = (_sigmoid(bo) * y * (bz * _sigmoid(bz))).astype(BF16)

            w_log = blast - brow + irow
            m_new = jnp.maximum(blast + m_prev, jnp.max(w_log, axis=-1, keepdims=True))
            a_vals.append(jnp.exp(blast + m_prev - m_new))
            w_row = jnp.exp(w_log - m_new)
            c_upd.append(jnp.dot((v_t * w_row).astype(BF16), k_halves[half], preferred_element_type=F32))
            n_upd.append(jnp.dot(jnp.broadcast_to(w_row, (2 * SUBLANES, L)).astype(BF16), k_halves[half],
                                 preferred_element_type=F32)[0:1])
            m_ref[h] = m_new

        a_row = jnp.where(lane1 < QK_DIM, a_vals[0], a_vals[1])
        c_ref[pr] = a_row * ct_pair + c_upd[0] + c_upd[1]
        n_ref[pr] = a_row * n_pair_row + n_upd[0] + n_upd[1]

    def cat(ref):
        return jnp.concatenate([ref[h] for h in range(HEADS)], axis=1)

    ya = jnp.dot(cat(oa_ref), wa_ref[...], preferred_element_type=F32)
    yb = jnp.dot(jnp.concatenate(hb, axis=1), wb_ref[...], preferred_element_type=F32)
    u = _sigmoid(cat(ga_ref).astype(F32)) * ya + _sigmoid(cat(gb_ref).astype(F32)) * yb
    o_ref[...] = x_ref[...] + jnp.dot(u.astype(BF16), wo_ref[...], preferred_element_type=F32)


def _mlstm_out(p, gates, gate_bias, conv_w, conv_b, norm_g, oa, x2, wa, wb, wo, *, layer, bsz, seq, chunk):
    m = bsz * seq
    nc = seq // chunk

    def slab_map(group):
        return lambda b, c: (group, b * nc + c, 0)

    w_spec = pl.BlockSpec((None, D_MODEL, D_MODEL), lambda b, c: (layer, 0, 0))
    return pl.pallas_call(
        functools.partial(_mlstm_kernel, chunk=chunk),
        grid=(bsz, nc),
        in_specs=[
            pl.BlockSpec((HEADS, chunk, LANES), slab_map(SLAB_BQK // HEADS)),
            pl.BlockSpec((HEADS, chunk, LANES), slab_map(SLAB_BV // HEADS)),
            pl.BlockSpec((HEADS, chunk, LANES), slab_map(SLAB_BO // HEADS)),
            pl.BlockSpec((HEADS, chunk, LANES), slab_map(SLAB_BZ // HEADS)),
            pl.BlockSpec((chunk, LANES), lambda b, c: (b * nc + c, 0)),
            pl.BlockSpec((1, LANES), lambda b, c: (0, 0)),
            pl.BlockSpec((CONV_WIDTH * 8, LANES), lambda b, c: (0, 0)),
            pl.BlockSpec((8, LANES), lambda b, c: (0, 0)),
            pl.BlockSpec((1, V_DIM), lambda b, c: (0, 0)),
            pl.BlockSpec((HEADS, chunk, LANES), slab_map(0)),
            pl.BlockSpec((HEADS, chunk, LANES), slab_map(SLAB_GA // HEADS)),
            pl.BlockSpec((HEADS, chunk, LANES), slab_map(SLAB_GB // HEADS)),
            pl.BlockSpec((chunk, D_MODEL), lambda b, c: (b * nc + c, 0)),
            w_spec, w_spec, w_spec,
        ],
        out_specs=pl.BlockSpec((chunk, D_MODEL), lambda b, c: (b * nc + c, 0)),
        out_shape=jax.ShapeDtypeStruct((m, D_MODEL), F32),
        scratch_shapes=[
            pltpu.VMEM((HEADS, SUBLANES + chunk, LANES), F32),
            pltpu.VMEM((HEADS // 2, V_DIM, LANES), F32),
            pltpu.VMEM((HEADS // 2, 1, LANES), F32),
            pltpu.VMEM((HEADS, 1, 1), F32),
        ],
        compiler_params=pltpu.CompilerParams(
            dimension_semantics=("arbitrary", "arbitrary"), vmem_limit_bytes=VMEM_LIMIT),
        name="mlstm_out",
    )(p, p, p, p, gates, gate_bias, conv_w, conv_b, norm_g, oa, p, p, x2, wa, wb, wo)


def _reorder_w_in(w_in):
    idx = [0]
    for s in _REF_SPLITS:
        idx.append(idx[-1] + s)
    gate0, gate1 = idx[6], idx[8]
    n_qk = idx[2]
    w_in = w_in.astype(BF16)
    w_qk = w_in[:, :, :n_qk].reshape(w_in.shape[:2] + (n_qk // LANES, 2, 2, ROPE_HALF))
    w_qk = jnp.swapaxes(w_qk, 3, 4).reshape(w_in.shape[:2] + (n_qk,))
    w_main = jnp.concatenate([w_qk, w_in[:, :, n_qk:gate0], w_in[:, :, gate1:]], axis=2)
    pad = jnp.zeros(w_in.shape[:2] + (LANES - (gate1 - gate0),), BF16)
    w_gate = jnp.concatenate([w_in[:, :, gate0:gate1], pad], axis=2)
    return w_main, w_gate


def _rope_lanes(per_dim):
    lo, hi = per_dim[:ROPE_HALF], per_dim[ROPE_HALF:]
    return jnp.concatenate([lo, lo, hi, hi])


def _pick(n, pref):
    return pref if n % pref == 0 else n


def kernel(x, positions, norm_g, w_in, q_norm_g, k_norm_g, lambda_qk, attn_norm_g, w_out_a,
           conv_w, conv_b, igate_b, fgate_b, mlstm_norm_g, w_out_b, w_o):
    bsz, seq, _ = x.shape
    m = bsz * seq
    depth = w_in.shape[0]
    tm = _pick(m, 1024)
    blk = _pick(seq, 512)
    chunk = _pick(seq, 256)

    inv_freq = ROPE_THETA ** (-jnp.arange(0, QK_DIM, 2, dtype=F32) / QK_DIM)
    ang = positions.astype(F32).reshape(m, 1) * inv_freq[None, :]
    cos, sin = jnp.cos(ang), jnp.sin(ang)
    cos4 = jnp.concatenate([cos, cos, cos, cos], axis=1)
    sin4 = jnp.concatenate([-sin, -sin, sin, sin], axis=1)
    lanes2 = jnp.arange(2 * LANES)
    grp = 2 * (lanes2 // LANES) + (lanes2 // ROPE_HALF) % 2
    ones_blk = (grp[:, None] == grp[None, :]).astype(BF16)

    x2 = x.reshape(m, D_MODEL)
    w_main, w_gate = _reorder_w_in(w_in)
    wa, wb, wo = w_out_a.astype(BF16), w_out_b.astype(BF16), w_o.astype(BF16)
    for l in range(depth):
        lam_init = 0.8 - 0.6 * math.exp(-0.3 * l)
        p, gates = _inproj(x2, norm_g[l][None, :], w_main, w_gate, layer=l, tm=tm, tn=2560)

        gq = _rope_lanes(q_norm_g[l]) * (QK_DIM ** -0.5 * LOG2_E)
        gk = _rope_lanes(k_norm_g[l])
        gains = jnp.concatenate([gq, gk])[None, :]
        q, k, vt = _attnprep(p, cos4, sin4, gains, ones_blk, tm=_pick(m, 4096), blk=blk)
        oa = _attention(q, k, vt, p, lambda_qk[l], attn_norm_g[l][None, :],
                        bsz=bsz, seq=seq, blk=blk, lam_init=lam_init)

        gate_bias = jnp.concatenate([igate_b[l], fgate_b[l], jnp.zeros((LANES - 2 * HEADS,), F32)])[None, :]
        cw = conv_w[l].reshape(CONV_WIDTH * 8, LANES)
        cb = conv_b[l].reshape(8, LANES)
        x2 = _mlstm_out(p, gates, gate_bias, cw, cb, mlstm_norm_g[l][None, :], oa, x2, wa, wb, wo,
                        layer=l, bsz=bsz, seq=seq, chunk=chunk)
    return x2.reshape(bsz, seq, D_MODEL)
```

```python
import functools
import math

import jax
import jax.numpy as jnp
from jax import lax
from jax.experimental import pallas as pl
from jax.experimental.pallas import tpu as pltpu

F32 = jnp.float32
BF16 = jnp.bfloat16

D_MODEL = 1024
HEADS = 8
QK_DIM = 64
V_DIM = 128
ROPE_HALF = QK_DIM // 2
CONV_WIDTH = 4
ROPE_THETA = 10000.0
EPS = 1e-6
NEG_INIT = -1e30
LOG2_E = math.log2(math.e)
LANES = 128
SUBLANES = 8
VMEM_LIMIT = 48 * 1024 * 1024

SLAB_AQ, SLAB_AK, SLAB_AV, SLAB_AZ = 0, 8, 16, 24
SLAB_BQK, SLAB_BV, SLAB_BO, SLAB_BZ, SLAB_GA, SLAB_GB = 32, 40, 48, 56, 64, 72
N_SLABS = 80
N_MAIN = N_SLABS * LANES

_REF_SPLITS = (1024, 1024, 1024, 1024, 1024, 1024, 8, 8, 1024, 1024, 1024, 1024)


def _split_hi_lo(x):
    hi = x.astype(BF16)
    lo = (x - hi.astype(F32)).astype(BF16)
    return hi, lo


def _sigmoid(x):
    return 1.0 / (1.0 + jnp.exp2(x * (-LOG2_E)))


def _inproj_kernel(x_ref, g_ref, w_ref, wg_ref, p_ref, gate_ref, h_ref, *, n_sub):
    @pl.when(pl.program_id(1) == 0)
    def _():
        x = x_ref[...]
        ms = jnp.mean(x * x, axis=-1, keepdims=True)
        h = ((x * lax.rsqrt(ms + EPS)) * g_ref[...]).astype(BF16)
        h_ref[...] = h
        gate_ref[...] = jnp.dot(h, wg_ref[...], preferred_element_type=F32)

    acc = jnp.dot(h_ref[...], w_ref[...], preferred_element_type=F32)
    for c in range(n_sub):
        p_ref[c] = acc[:, c * LANES:(c + 1) * LANES].astype(BF16)


def _inproj(x2, norm_g, w_main, w_gate, *, layer, tm, tn):
    m = x2.shape[0]
    n_sub = tn // LANES
    return pl.pallas_call(
        functools.partial(_inproj_kernel, n_sub=n_sub),
        grid=(m // tm, N_MAIN // tn),
        in_specs=[
            pl.BlockSpec((tm, D_MODEL), lambda i, j: (i, 0)),
            pl.BlockSpec((1, D_MODEL), lambda i, j: (0, 0)),
            pl.BlockSpec((None, D_MODEL, tn), lambda i, j: (layer, 0, j)),
            pl.BlockSpec((None, D_MODEL, LANES), lambda i, j: (layer, 0, 0)),
        ],
        out_specs=[
            pl.BlockSpec((n_sub, tm, LANES), lambda i, j: (j, i, 0)),
            pl.BlockSpec((tm, LANES), lambda i, j: (i, 0)),
        ],
        out_shape=[
            jax.ShapeDtypeStruct((N_SLABS, m, LANES), BF16),
            jax.ShapeDtypeStruct((m, LANES), F32),
        ],
        scratch_shapes=[pltpu.VMEM((tm, D_MODEL), BF16)],
        compiler_params=pltpu.CompilerParams(
            dimension_semantics=("arbitrary", "arbitrary"), vmem_limit_bytes=VMEM_LIMIT),
        name="inproj",
    )(x2, norm_g, w_main, w_gate)


def _attnprep_kernel(q_ref, k_ref, v_ref, cos_ref, sin_ref, g_ref, ones_ref, qt_ref, ko_ref, vt_ref,
                     *, blk, n_sub):
    x = jnp.concatenate([q_ref[...], k_ref[...]], axis=1).astype(F32)
    hi, lo = _split_hi_lo(x * x)
    ones = ones_ref[...]
    ss = (jnp.dot(hi, ones, preferred_element_type=F32)
          + jnp.dot(lo, ones, preferred_element_type=F32))
    y = (x * lax.rsqrt(ss * (1.0 / QK_DIM) + EPS)) * g_ref[...]
    cos = cos_ref[...]
    sin = sin_ref[...]
    roped = []
    for side in range(2):
        ys = y[:, side * LANES:(side + 1) * LANES]
        roped.append(ys * cos + pltpu.roll(ys, LANES // 2, axis=1) * sin)
    ko_ref[...] = roped[1].astype(BF16)
    for c in range(n_sub):
        rows = slice(c * blk, (c + 1) * blk)
        qt_ref[c] = jnp.transpose(roped[0][rows]).astype(BF16)
        vt_ref[c] = jnp.transpose(v_ref[rows, :].astype(F32)).astype(BF16)


def _attnprep(p, cos4, sin4, gains, ones_blk, *, tm, blk):
    m = p.shape[1]
    n_sub = tm // blk
    return pl.pallas_call(
        functools.partial(_attnprep_kernel, blk=blk, n_sub=n_sub),
        grid=(m // tm, HEADS),
        in_specs=[
            pl.BlockSpec((None, tm, LANES), lambda i, h: (SLAB_AQ + h, i, 0)),
            pl.BlockSpec((None, tm, LANES), lambda i, h: (SLAB_AK + h, i, 0)),
            pl.BlockSpec((None, tm, LANES), lambda i, h: (SLAB_AV + h, i, 0)),
            pl.BlockSpec((tm, LANES), lambda i, h: (i, 0)),
            pl.BlockSpec((tm, LANES), lambda i, h: (i, 0)),
            pl.BlockSpec((1, 2 * LANES), lambda i, h: (0, 0)),
            pl.BlockSpec((2 * LANES, 2 * LANES), lambda i, h: (0, 0)),
        ],
        out_specs=[
            pl.BlockSpec((None, n_sub, LANES, blk), lambda i, h: (h, i, 0, 0)),
            pl.BlockSpec((None, tm, LANES), lambda i, h: (h, i, 0)),
            pl.BlockSpec((None, n_sub, V_DIM, blk), lambda i, h: (h, i, 0, 0)),
        ],
        out_shape=[
            jax.ShapeDtypeStruct((HEADS, m // blk, LANES, blk), BF16),
            jax.ShapeDtypeStruct((HEADS, m, LANES), BF16),
            jax.ShapeDtypeStruct((HEADS, m // blk, V_DIM, blk), BF16),
        ],
        compiler_params=pltpu.CompilerParams(
            dimension_semantics=("arbitrary", "arbitrary"), vmem_limit_bytes=VMEM_LIMIT),
        name="attnprep",
    )(p, p, p, cos4, sin4, gains, ones_blk)


def _attn_kernel(q_ref, k_ref, vt_ref, az_ref, lam_ref, g_ref, o_ref,
                 qs_ref, sa_ref, sb_ref, ca_ref, cb_ref, m_ref, l_ref, acc_ref, *, blk, lam_init):
    nb = q_ref.shape[0]

    def produce(kc, s_ref, c_ref, masked):
        k = k_ref[pl.ds(pl.multiple_of(kc * blk, blk), blk), :]
        s = jnp.dot(k, qs_ref[...], preferred_element_type=F32)
        if masked:
            kpos = lax.broadcasted_iota(jnp.int32, s.shape, 0)
            qpos = lax.broadcasted_iota(jnp.int32, s.shape, 1) & (blk - 1)
            s = jnp.where(kpos <= qpos, s, -jnp.inf)
        s_ref[...] = s
        c_ref[...] = jnp.max(s, axis=0, keepdims=True)

    def consume(kc, s_ref, c_ref):
        m = m_ref[...]
        m_new = jnp.maximum(m, c_ref[...])
        alpha = jnp.exp2(m - m_new)
        p = jnp.exp2(s_ref[...] - m_new)
        l_ref[...] = alpha * l_ref[...] + jnp.sum(p, axis=0, keepdims=True)
        pv = jnp.dot(vt_ref[kc], p.astype(BF16), preferred_element_type=F32)
        acc_ref[...] = acc_ref[...] * alpha + pv
        m_ref[...] = m_new

    buf_a = (sa_ref, ca_ref)
    buf_b = (sb_ref, cb_ref)

    def two_chunks(j):
        produce(j + 1, *buf_b, False)
        consume(j, *buf_a)
        produce(j + 2, *buf_a, False)
        consume(j + 1, *buf_b)

    def pair(t, carry):
        two_chunks(2 * t)
        return carry

    def quad(t, carry):
        two_chunks(4 * t)
        two_chunks(4 * t + 2)
        return carry

    def octet(t, carry):
        for c in range(0, 8, 2):
            two_chunks(8 * t + c)
        return carry

    lq = lam_ref[...]
    lam = (jnp.exp(jnp.sum(lq[0:1] * lq[1:2], axis=-1, keepdims=True))
           - jnp.exp(jnp.sum(lq[2:3] * lq[3:4], axis=-1, keepdims=True)) + lam_init)

    def setup(qi):
        q_t = q_ref[qi]
        dim = lax.broadcasted_iota(jnp.int32, q_t.shape, 0)
        zero = jnp.zeros_like(q_t)
        map0 = (dim & ROPE_HALF) == 0
        qs_ref[:, 0:blk] = jnp.where(map0, q_t, zero)
        qs_ref[:, blk:2 * blk] = jnp.where(map0, zero, q_t)
        acc_ref[...] = jnp.zeros(acc_ref.shape, F32)
        m_ref[...] = jnp.full(m_ref.shape, -jnp.inf, F32)
        l_ref[...] = jnp.zeros(l_ref.shape, F32)

    def finalize(qi):
        rows = pl.ds(pl.multiple_of(qi * blk, blk), blk)
        o_t = acc_ref[...] / l_ref[...]
        o = jnp.transpose(o_t[:, 0:blk] - lam * o_t[:, blk:2 * blk])
        ms = jnp.mean(o * o, axis=-1, keepdims=True)
        y = (o * lax.rsqrt(ms + EPS)) * g_ref[...] * (1.0 - lam_init)
        az = az_ref[rows, :].astype(F32)
        o_ref[rows, :] = (y * (az * _sigmoid(az))).astype(BF16)

    setup(0)
    produce(0, *buf_a, True)
    consume(0, *buf_a)

    def qblock(qi, carry):
        finalize(qi - 1)
        setup(qi)
        produce(0, *buf_a, False)

        n_pairs = (qi - 1) // 2
        n_quads = n_pairs // 2
        n_octets = n_quads // 2
        lax.fori_loop(0, n_octets, octet, 0)
        lax.fori_loop(2 * n_octets, n_quads, quad, 0)
        lax.fori_loop(2 * n_quads, n_pairs, pair, 0)
        rest = qi - 2 * n_pairs

        @pl.when(rest == 1)
        def _():
            produce(qi, *buf_b, True)
            consume(qi - 1, *buf_a)
            consume(qi, *buf_b)

        @pl.when(rest == 2)
        def _():
            produce(qi - 1, *buf_b, False)
            consume(qi - 2, *buf_a)
            produce(qi, *buf_a, True)
            consume(qi - 1, *buf_b)
            consume(qi, *buf_a)

        return carry

    lax.fori_loop(1, nb, qblock, 0)
    finalize(nb - 1)


def _attention(q, k, vt, p, lambda_qk, attn_norm_g, *, bsz, seq, blk, lam_init):
    m = bsz * seq
    nb = seq // blk

    return pl.pallas_call(
        functools.partial(_attn_kernel, blk=blk, lam_init=lam_init),
        grid=(bsz * HEADS,),
        in_specs=[
            pl.BlockSpec((None, nb, LANES, blk), lambda bh: (bh % HEADS, bh // HEADS, 0, 0)),
            pl.BlockSpec((None, seq, LANES), lambda bh: (bh % HEADS, bh // HEADS, 0)),
            pl.BlockSpec((None, nb, V_DIM, blk), lambda bh: (bh % HEADS, bh // HEADS, 0, 0)),
            pl.BlockSpec((None, seq, LANES), lambda bh: (SLAB_AZ + bh % HEADS, bh // HEADS, 0)),
            pl.BlockSpec((4, QK_DIM), lambda bh: (0, 0)),
            pl.BlockSpec((1, V_DIM), lambda bh: (0, 0)),
        ],
        out_specs=pl.BlockSpec((None, seq, LANES), lambda bh: (bh % HEADS, bh // HEADS, 0)),
        out_shape=jax.ShapeDtypeStruct((HEADS, m, LANES), BF16),
        scratch_shapes=[
            pltpu.VMEM((LANES, 2 * blk), BF16),
            pltpu.VMEM((blk, 2 * blk), F32),
            pltpu.VMEM((blk, 2 * blk), F32),
            pltpu.VMEM((1, 2 * blk), F32),
            pltpu.VMEM((1, 2 * blk), F32),
            pltpu.VMEM((1, 2 * blk), F32),
            pltpu.VMEM((1, 2 * blk), F32),
            pltpu.VMEM((V_DIM, 2 * blk), F32),
        ],
        compiler_params=pltpu.CompilerParams(
            dimension_semantics=("arbitrary",), vmem_limit_bytes=VMEM_LIMIT),
        name="diffattn",
    )(q, k, vt, p, lambda_qk, attn_norm_g)


def _log_sigmoid(x):
    return jnp.minimum(x, 0.0) - jnp.log(1.0 + jnp.exp(-jnp.abs(x)))


def _mlstm_kernel(qk_ref, v_ref, bo_ref, bz_ref, gate_ref, gbias_ref, cw_ref, cb_ref, ng_ref,
                  oa_ref, ga_ref, gb_ref, x_ref, wa_ref, wb_ref, wo_ref,
                  o_ref, tail_ref, c_ref, n_ref, m_ref, *, chunk):
    L = chunk
    n_pair = HEADS // 2

    @pl.when(pl.program_id(1) == 0)
    def _():
        tail_ref[...] = jnp.zeros(tail_ref.shape, F32)
        c_ref[...] = jnp.zeros(c_ref.shape, F32)
        n_ref[...] = jnp.zeros(n_ref.shape, F32)
        m_ref[...] = jnp.full(m_ref.shape, NEG_INIT, F32)

    conv = []
    for sl in range(2 * n_pair):
        x = qk_ref[sl].astype(F32)
        tail_ref[sl, SUBLANES:SUBLANES + L, :] = x
        y = x * cw_ref[(CONV_WIDTH - 1) * 8 + sl: (CONV_WIDTH - 1) * 8 + sl + 1, :] + cb_ref[sl:sl + 1, :]
        for d in range(1, CONV_WIDTH):
            j = CONV_WIDTH - 1 - d
            y = y + tail_ref[sl, SUBLANES - d:SUBLANES - d + L, :] * cw_ref[j * 8 + sl: j * 8 + sl + 1, :]
        tail_ref[sl, 0:SUBLANES, :] = x[L - SUBLANES:L]
        conv.append(y * _sigmoid(y))
    lane = lax.broadcasted_iota(jnp.int32, (L, LANES), 1)
    lo_half = lane < QK_DIM

    g = gate_ref[...] + gbias_ref[...]
    ls = _log_sigmoid(g)
    r_i = lax.broadcasted_iota(jnp.int32, (L, L), 0)
    c_i = lax.broadcasted_iota(jnp.int32, (L, L), 1)
    tri = jnp.where(c_i <= r_i, 1.0, 0.0).astype(BF16)
    ls_hi, ls_lo = _split_hi_lo(ls)
    bcum = (jnp.dot(tri, ls_hi, preferred_element_type=F32)
            + jnp.dot(tri, ls_lo, preferred_element_type=F32))
    g_t = jnp.transpose(g)
    b_t = jnp.transpose(bcum)
    causal_t = r_i <= c_i
    lane1 = lax.broadcasted_iota(jnp.int32, (1, LANES), 1)
    nt = (((1,), (1,)), ((), ()))

    hb = [None] * HEADS
    for pr in range(n_pair):
        q_bf = (conv[pr] * (QK_DIM ** -0.5)).astype(BF16)
        k_bf = conv[n_pair + pr].astype(BF16)
        zero = jnp.zeros_like(q_bf)
        q_stack = jnp.concatenate([jnp.where(lo_half, q_bf, zero), jnp.where(lo_half, zero, q_bf)], axis=0)
        k_halves = (jnp.where(lo_half, k_bf, zero), jnp.where(lo_half, zero, k_bf))
        s_pair = lax.dot_general(k_bf, q_stack, nt, preferred_element_type=F32)
        ct_pair = c_ref[pr]
        inter_pair = lax.dot_general(ct_pair.astype(BF16), q_stack, nt, preferred_element_type=F32)
        n_pair_row = n_ref[pr]
        qn_pair = lax.dot_general(jnp.broadcast_to(n_pair_row, (2 * SUBLANES, LANES)).astype(BF16), q_stack, nt,
                                  preferred_element_type=F32)[0:1]

        a_vals, c_upd, n_upd = [], [], []
        for half in range(2):
            h = 2 * pr + half
            cols = slice(half * L, (half + 1) * L)
            ccol = g[:, h: h + 1] - bcum[:, HEADS + h: HEADS + h + 1]
            brow = b_t[HEADS + h: HEADS + h + 1, :]
            irow = g_t[h: h + 1, :]
            blast = brow[:, L - 1: L]
            m_prev = m_ref[h]

            d_t = jnp.where(causal_t, ccol + brow, -jnp.inf)
            inter_log = brow + m_prev
            m_t = jnp.maximum(inter_log, jnp.max(d_t, axis=0, keepdims=True))
            sc = s_pair[:, cols] * jnp.exp(d_t - m_t)
            inter_w = jnp.exp(inter_log - m_t)
            v_t = jnp.transpose(v_ref[h].astype(F32))
            num = (jnp.dot(v_t.astype(BF16), sc.astype(BF16), preferred_element_type=F32)
                   + inter_w * inter_pair[:, cols])
            den = jnp.sum(sc, axis=0, keepdims=True) + inter_w * qn_pair[:, cols]
            h_t = num / jnp.maximum(jnp.abs(den), jnp.exp(-m_t))

            ms = jnp.mean(h_t * h_t, axis=0, keepdims=True)
            y = jnp.transpose(h_t * lax.rsqrt(ms + EPS)) * ng_ref[...]
            bo = bo_ref[h].astype(F32)
            bz = bz_ref[h].astype(F32)
            hb[h] = (_sigmoid(bo) * y * (bz * _sigmoid(bz))).astype(BF16)

            w_log = blast - brow + irow
            m_new = jnp.maximum(blast + m_prev, jnp.max(w_log, axis=-1, keepdims=True))
            a_vals.append(jnp.exp(blast + m_prev - m_new))
            w_row = jnp.exp(w_log - m_new)
            c_upd.append(jnp.dot((v_t * w_row).astype(BF16), k_halves[half], preferred_element_type=F32))
            n_upd.append(jnp.dot(jnp.broadcast_to(w_row, (2 * SUBLANES, L)).astype(BF16), k_halves[half],
                                 preferred_element_type=F32)[0:1])
            m_ref[h] = m_new

        a_row = jnp.where(lane1 < QK_DIM, a_vals[0], a_vals[1])
        c_ref[pr] = a_row * ct_pair + c_upd[0] + c_upd[1]
        n_ref[pr] = a_row * n_pair_row + n_upd[0] + n_upd[1]

    def cat(ref):
        return jnp.concatenate([ref[h] for h in range(HEADS)], axis=1)

    ya = jnp.dot(cat(oa_ref), wa_ref[...], preferred_element_type=F32)
    yb = jnp.dot(jnp.concatenate(hb, axis=1), wb_ref[...], preferred_element_type=F32)
    u = _sigmoid(cat(ga_ref).astype(F32)) * ya + _sigmoid(cat(gb_ref).astype(F32)) * yb
    o_ref[...] = x_ref[...] + jnp.dot(u.astype(BF16), wo_ref[...], preferred_element_type=F32)


def _mlstm_out(p, gates, gate_bias, conv_w, conv_b, norm_g, oa, x2, wa, wb, wo, *, layer, bsz, seq, chunk):
    m = bsz * seq
    nc = seq // chunk

    def slab_map(group):
        return lambda b, c: (group, b * nc + c, 0)

    w_spec = pl.BlockSpec((None, D_MODEL, D_MODEL), lambda b, c: (layer, 0, 0))
    return pl.pallas_call(
        functools.partial(_mlstm_kernel, chunk=chunk),
        grid=(bsz, nc),
        in_specs=[
            pl.BlockSpec((HEADS, chunk, LANES), slab_map(SLAB_BQK // HEADS)),
            pl.BlockSpec((HEADS, chunk, LANES), slab_map(SLAB_BV // HEADS)),
            pl.BlockSpec((HEADS, chunk, LANES), slab_map(SLAB_BO // HEADS)),
            pl.BlockSpec((HEADS, chunk, LANES), slab_map(SLAB_BZ // HEADS)),
            pl.BlockSpec((chunk, LANES), lambda b, c: (b * nc + c, 0)),
            pl.BlockSpec((1, LANES), lambda b, c: (0, 0)),
            pl.BlockSpec((CONV_WIDTH * 8, LANES), lambda b, c: (0, 0)),
            pl.BlockSpec((8, LANES), lambda b, c: (0, 0)),
            pl.BlockSpec((1, V_DIM), lambda b, c: (0, 0)),
            pl.BlockSpec((HEADS, chunk, LANES), slab_map(0)),
            pl.BlockSpec((HEADS, chunk, LANES), slab_map(SLAB_GA // HEADS)),
            pl.BlockSpec((HEADS, chunk, LANES), slab_map(SLAB_GB // HEADS)),
            pl.BlockSpec((chunk, D_MODEL), lambda b, c: (b * nc + c, 0)),
            w_spec, w_spec, w_spec,
        ],
        out_specs=pl.BlockSpec((chunk, D_MODEL), lambda b, c: (b * nc + c, 0)),
        out_shape=jax.ShapeDtypeStruct((m, D_MODEL), F32),
        scratch_shapes=[
            pltpu.VMEM((HEADS, SUBLANES + chunk, LANES), F32),
            pltpu.VMEM((HEADS // 2, V_DIM, LANES), F32),
            pltpu.VMEM((HEADS // 2, 1, LANES), F32),
            pltpu.VMEM((HEADS, 1, 1), F32),
        ],
        compiler_params=pltpu.CompilerParams(
            dimension_semantics=("arbitrary", "arbitrary"), vmem_limit_bytes=VMEM_LIMIT),
        name="mlstm_out",
    )(p, p, p, p, gates, gate_bias, conv_w, conv_b, norm_g, oa, p, p, x2, wa, wb, wo)


def _reorder_w_in(w_in):
    idx = [0]
    for s in _REF_SPLITS:
        idx.append(idx[-1] + s)
    gate0, gate1 = idx[6], idx[8]
    n_qk = idx[2]
    w_in = w_in.astype(BF16)
    w_qk = w_in[:, :, :n_qk].reshape(w_in.shape[:2] + (n_qk // LANES, 2, 2, ROPE_HALF))
    w_qk = jnp.swapaxes(w_qk, 3, 4).reshape(w_in.shape[:2] + (n_qk,))
    w_main = jnp.concatenate([w_qk, w_in[:, :, n_qk:gate0], w_in[:, :, gate1:]], axis=2)
    pad = jnp.zeros(w_in.shape[:2] + (LANES - (gate1 - gate0),), BF16)
    w_gate = jnp.concatenate([w_in[:, :, gate0:gate1], pad], axis=2)
    return w_main, w_gate


def _rope_lanes(per_dim):
    lo, hi = per_dim[:ROPE_HALF], per_dim[ROPE_HALF:]
    return jnp.concatenate([lo, lo, hi, hi])


def _pick(n, pref):
    return pref if n % pref == 0 else n


def kernel(x, positions, norm_g, w_in, q_norm_g, k_norm_g, lambda_qk, attn_norm_g, w_out_a,
           conv_w, conv_b, igate_b, fgate_b, mlstm_norm_g, w_out_b, w_o):
    bsz, seq, _ = x.shape
    m = bsz * seq
    depth = w_in.shape[0]
    tm = _pick(m, 1024)
    blk = _pick(seq, 512)
    chunk = _pick(seq, 256)

    inv_freq = ROPE_THETA ** (-jnp.arange(0, QK_DIM, 2, dtype=F32) / QK_DIM)
    ang = positions.astype(F32).reshape(m, 1) * inv_freq[None, :]
    cos, sin = jnp.cos(ang), jnp.sin(ang)
    cos4 = jnp.concatenate([cos, cos, cos, cos], axis=1)
    sin4 = jnp.concatenate([-sin, -sin, sin, sin], axis=1)
    lanes2 = jnp.arange(2 * LANES)
    grp = 2 * (lanes2 // LANES) + (lanes2 // ROPE_HALF) % 2
    ones_blk = (grp[:, None] == grp[None, :]).astype(BF16)

    x2 = x.reshape(m, D_MODEL)
    w_main, w_gate = _reorder_w_in(w_in)
    wa, wb, wo = w_out_a.astype(BF16), w_out_b.astype(BF16), w_o.astype(BF16)
    for l in range(depth):
        lam_init = 0.8 - 0.6 * math.exp(-0.3 * l)
        p, gates = _inproj(x2, norm_g[l][None, :], w_main, w_gate, layer=l, tm=tm, tn=2560)

        gq = _rope_lanes(q_norm_g[l]) * (QK_DIM ** -0.5 * LOG2_E)
        gk = _rope_lanes(k_norm_g[l])
        gains = jnp.concatenate([gq, gk])[None, :]
        q, k, vt = _attnprep(p, cos4, sin4, gains, ones_blk, tm=_pick(m, 4096), blk=blk)
        oa = _attention(q, k, vt, p, lambda_qk[l], attn_norm_g[l][None, :],
                        bsz=bsz, seq=seq, blk=blk, lam_init=lam_init)

        gate_bias = jnp.concatenate([igate_b[l], fgate_b[l], jnp.zeros((LANES - 2 * HEADS,), F32)])[None, :]
        cw = conv_w[l].reshape(CONV_WIDTH * 8, LANES)
        cb = conv_b[l].reshape(8, LANES)
        x2 = _mlstm_out(p, gates, gate_bias, cw, cb, mlstm_norm_g[l][None, :], oa, x2, wa, wb, wo,
                        layer=l, bsz=bsz, seq=seq, chunk=chunk)
    return x2.reshape(bsz, seq, D_MODEL)
```

```python
import functools
import math

import jax
import jax.numpy as jnp
from jax import lax
from jax.experimental import pallas as pl
from jax.experimental.pallas import tpu as pltpu

F32 = jnp.float32
BF16 = jnp.bfloat16

D_MODEL = 1024
HEADS = 8
QK_DIM = 64
V_DIM = 128
V_EXT = V_DIM + 16
ROPE_HALF = QK_DIM // 2
CONV_WIDTH = 4
ROPE_THETA = 10000.0
EPS = 1e-6
NEG_INIT = -1e30
LOG2_E = math.log2(math.e)
LANES = 128
SUBLANES = 8
VMEM_LIMIT = 48 * 1024 * 1024

SLAB_AQ, SLAB_AK, SLAB_AV, SLAB_AZ = 0, 8, 16, 24
SLAB_BQK, SLAB_BV, SLAB_BO, SLAB_BZ, SLAB_GA, SLAB_GB = 32, 40, 48, 56, 64, 72
N_SLABS = 80
N_MAIN = N_SLABS * LANES

_REF_SPLITS = (1024, 1024, 1024, 1024, 1024, 1024, 8, 8, 1024, 1024, 1024, 1024)


def _split_hi_lo(x):
    hi = x.astype(BF16)
    lo = (x - hi.astype(F32)).astype(BF16)
    return hi, lo


def _sigmoid(x):
    return 1.0 / (1.0 + jnp.exp2(x * (-LOG2_E)))


def _inproj_kernel(x_ref, g_ref, w_ref, wg_ref, p_ref, gate_ref, h_ref, *, n_sub):
    @pl.when(pl.program_id(1) == 0)
    def _():
        x = x_ref[...]
        ms = jnp.mean(x * x, axis=-1, keepdims=True)
        h = ((x * lax.rsqrt(ms + EPS)) * g_ref[...]).astype(BF16)
        h_ref[...] = h
        gate_ref[...] = jnp.dot(h, wg_ref[...], preferred_element_type=F32)

    acc = jnp.dot(h_ref[...], w_ref[...], preferred_element_type=F32)
    for c in range(n_sub):
        p_ref[c] = acc[:, c * LANES:(c + 1) * LANES].astype(BF16)


def _inproj(x2, norm_g, w_main, w_gate, *, layer, tm, tn):
    m = x2.shape[0]
    n_sub = tn // LANES
    return pl.pallas_call(
        functools.partial(_inproj_kernel, n_sub=n_sub),
        grid=(m // tm, N_MAIN // tn),
        in_specs=[
            pl.BlockSpec((tm, D_MODEL), lambda i, j: (i, 0)),
            pl.BlockSpec((1, D_MODEL), lambda i, j: (0, 0)),
            pl.BlockSpec((None, D_MODEL, tn), lambda i, j: (layer, 0, j)),
            pl.BlockSpec((None, D_MODEL, LANES), lambda i, j: (layer, 0, 0)),
        ],
        out_specs=[
            pl.BlockSpec((n_sub, tm, LANES), lambda i, j: (j, i, 0)),
            pl.BlockSpec((tm, LANES), lambda i, j: (i, 0)),
        ],
        out_shape=[
            jax.ShapeDtypeStruct((N_SLABS, m, LANES), BF16),
            jax.ShapeDtypeStruct((m, LANES), F32),
        ],
        scratch_shapes=[pltpu.VMEM((tm, D_MODEL), BF16)],
        compiler_params=pltpu.CompilerParams(
            dimension_semantics=("arbitrary", "arbitrary"), vmem_limit_bytes=VMEM_LIMIT),
        name="inproj",
    )(x2, norm_g, w_main, w_gate)


def _attnprep_kernel(q_ref, k_ref, v_ref, cos_ref, sin_ref, g_ref, ones_ref, qt_ref, ko_ref, vt_ref,
                     *, blk, n_sub):
    x = jnp.concatenate([q_ref[...], k_ref[...]], axis=1).astype(F32)
    hi, lo = _split_hi_lo(x * x)
    ones = ones_ref[...]
    ss = (jnp.dot(hi, ones, preferred_element_type=F32)
          + jnp.dot(lo, ones, preferred_element_type=F32))
    y = (x * lax.rsqrt(ss * (1.0 / QK_DIM) + EPS)) * g_ref[...]
    cos = cos_ref[...]
    sin = sin_ref[...]
    roped = []
    for side in range(2):
        ys = y[:, side * LANES:(side + 1) * LANES]
        roped.append(ys * cos + pltpu.roll(ys, LANES // 2, axis=1) * sin)
    ko_ref[...] = roped[1].astype(BF16)
    for c in range(n_sub):
        rows = slice(c * blk, (c + 1) * blk)
        qt_ref[c] = jnp.transpose(roped[0][rows]).astype(BF16)
        vt_ref[c, 0:V_DIM, :] = jnp.transpose(v_ref[rows, :].astype(F32)).astype(BF16)
        vt_ref[c, V_DIM:V_EXT, :] = jnp.ones((V_EXT - V_DIM, blk), BF16)


def _attnprep(p, cos4, sin4, gains, ones_blk, *, tm, blk):
    m = p.shape[1]
    n_sub = tm // blk
    return pl.pallas_call(
        functools.partial(_attnprep_kernel, blk=blk, n_sub=n_sub),
        grid=(m // tm, HEADS),
        in_specs=[
            pl.BlockSpec((None, tm, LANES), lambda i, h: (SLAB_AQ + h, i, 0)),
            pl.BlockSpec((None, tm, LANES), lambda i, h: (SLAB_AK + h, i, 0)),
            pl.BlockSpec((None, tm, LANES), lambda i, h: (SLAB_AV + h, i, 0)),
            pl.BlockSpec((tm, LANES), lambda i, h: (i, 0)),
            pl.BlockSpec((tm, LANES), lambda i, h: (i, 0)),
            pl.BlockSpec((1, 2 * LANES), lambda i, h: (0, 0)),
            pl.BlockSpec((2 * LANES, 2 * LANES), lambda i, h: (0, 0)),
        ],
        out_specs=[
            pl.BlockSpec((None, n_sub, LANES, blk), lambda i, h: (h, i, 0, 0)),
            pl.BlockSpec((None, tm, LANES), lambda i, h: (h, i, 0)),
            pl.BlockSpec((None, n_sub, V_EXT, blk), lambda i, h: (h, i, 0, 0)),
        ],
        out_shape=[
            jax.ShapeDtypeStruct((HEADS, m // blk, LANES, blk), BF16),
            jax.ShapeDtypeStruct((HEADS, m, LANES), BF16),
            jax.ShapeDtypeStruct((HEADS, m // blk, V_EXT, blk), BF16),
        ],
        compiler_params=pltpu.CompilerParams(
            dimension_semantics=("arbitrary", "arbitrary"), vmem_limit_bytes=VMEM_LIMIT),
        name="attnprep",
    )(p, p, p, cos4, sin4, gains, ones_blk)


def _attn_kernel(q_ref, k_ref, vt_ref, az_ref, lam_ref, g_ref, o_ref,
                 qs_ref, sa_ref, sb_ref, ca_ref, cb_ref, m_ref, acc_ref, *, blk, lam_init):
    nb = q_ref.shape[0]

    def produce(kc, s_ref, c_ref, masked):
        k = k_ref[pl.ds(pl.multiple_of(kc * blk, blk), blk), :]
        s = jnp.dot(k, qs_ref[...], preferred_element_type=F32)
        if masked:
            kpos = lax.broadcasted_iota(jnp.int32, s.shape, 0)
            qpos = lax.broadcasted_iota(jnp.int32, s.shape, 1) & (blk - 1)
            s = jnp.where(kpos <= qpos, s, -jnp.inf)
        s_ref[...] = s
        c_ref[...] = jnp.max(s, axis=0, keepdims=True)

    def consume(kc, s_ref, c_ref):
        m = m_ref[...]
        m_new = jnp.maximum(m, c_ref[...])
        alpha = jnp.exp2(m - m_new)
        p = jnp.exp2(s_ref[...] - m_new)
        pv = jnp.dot(vt_ref[kc], p.astype(BF16), preferred_element_type=F32)
        acc_ref[...] = acc_ref[...] * alpha + pv
        m_ref[...] = m_new

    buf_a = (sa_ref, ca_ref)
    buf_b = (sb_ref, cb_ref)

    def two_chunks(j):
        produce(j + 1, *buf_b, False)
        consume(j, *buf_a)
        produce(j + 2, *buf_a, False)
        consume(j + 1, *buf_b)

    def pair(t, carry):
        two_chunks(2 * t)
        return carry

    def quad(t, carry):
        two_chunks(4 * t)
        two_chunks(4 * t + 2)
        return carry

    lq = lam_ref[...]
    lam = (jnp.exp(jnp.sum(lq[0:1] * lq[1:2], axis=-1, keepdims=True))
           - jnp.exp(jnp.sum(lq[2:3] * lq[3:4], axis=-1, keepdims=True)) + lam_init)

    def setup(qi):
        q_t = q_ref[qi]
        dim = lax.broadcasted_iota(jnp.int32, q_t.shape, 0)
        zero = jnp.zeros_like(q_t)
        map0 = (dim & ROPE_HALF) == 0
        qs_ref[:, 0:blk] = jnp.where(map0, q_t, zero)
        qs_ref[:, blk:2 * blk] = jnp.where(map0, zero, q_t)
        acc_ref[...] = jnp.zeros(acc_ref.shape, F32)
        m_ref[...] = jnp.full(m_ref.shape, -jnp.inf, F32)

    def finalize(qi):
        rows = pl.ds(pl.multiple_of(qi * blk, blk), blk)
        o_t = acc_ref[0:V_DIM, :] / acc_ref[V_DIM:V_DIM + 1, :]
        o = jnp.transpose(o_t[:, 0:blk] - lam * o_t[:, blk:2 * blk])
        ms = jnp.mean(o * o, axis=-1, keepdims=True)
        y = (o * lax.rsqrt(ms + EPS)) * g_ref[...] * (1.0 - lam_init)
        az = az_ref[rows, :].astype(F32)
        o_ref[rows, :] = (y * (az * _sigmoid(az))).astype(BF16)

    setup(0)
    produce(0, *buf_a, True)
    consume(0, *buf_a)

    def qblock(qi, carry):
        finalize(qi - 1)
        setup(qi)
        produce(0, *buf_a, False)

        n_pairs = (qi - 1) // 2
        n_quads = n_pairs // 2
        lax.fori_loop(0, n_quads, quad, 0)
        lax.fori_loop(2 * n_quads, n_pairs, pair, 0)
        rest = qi - 2 * n_pairs

        @pl.when(rest == 1)
        def _():
            produce(qi, *buf_b, True)
            consume(qi - 1, *buf_a)
            consume(qi, *buf_b)

        @pl.when(rest == 2)
        def _():
            produce(qi - 1, *buf_b, False)
            consume(qi - 2, *buf_a)
            produce(qi, *buf_a, True)
            consume(qi - 1, *buf_b)
            consume(qi, *buf_a)

        return carry

    lax.fori_loop(1, nb, qblock, 0)
    finalize(nb - 1)


def _attention(q, k, vt, p, lambda_qk, attn_norm_g, *, bsz, seq, blk, lam_init):
    m = bsz * seq
    nb = seq // blk

    return pl.pallas_call(
        functools.partial(_attn_kernel, blk=blk, lam_init=lam_init),
        grid=(bsz * HEADS,),
        in_specs=[
            pl.BlockSpec((None, nb, LANES, blk), lambda bh: (bh % HEADS, bh // HEADS, 0, 0)),
            pl.BlockSpec((None, seq, LANES), lambda bh: (bh % HEADS, bh // HEADS, 0)),
            pl.BlockSpec((None, nb, V_EXT, blk), lambda bh: (bh % HEADS, bh // HEADS, 0, 0)),
            pl.BlockSpec((None, seq, LANES), lambda bh: (SLAB_AZ + bh % HEADS, bh // HEADS, 0)),
            pl.BlockSpec((4, QK_DIM), lambda bh: (0, 0)),
            pl.BlockSpec((1, V_DIM), lambda bh: (0, 0)),
        ],
        out_specs=pl.BlockSpec((None, seq, LANES), lambda bh: (bh % HEADS, bh // HEADS, 0)),
        out_shape=jax.ShapeDtypeStruct((HEADS, m, LANES), BF16),
        scratch_shapes=[
            pltpu.VMEM((LANES, 2 * blk), BF16),
            pltpu.VMEM((blk, 2 * blk), F32),
            pltpu.VMEM((blk, 2 * blk), F32),
            pltpu.VMEM((1, 2 * blk), F32),
            pltpu.VMEM((1, 2 * blk), F32),
            pltpu.VMEM((1, 2 * blk), F32),
            pltpu.VMEM((V_EXT, 2 * blk), F32),
        ],
        compiler_params=pltpu.CompilerParams(
            dimension_semantics=("arbitrary",), vmem_limit_bytes=VMEM_LIMIT),
        name="diffattn",
    )(q, k, vt, p, lambda_qk, attn_norm_g)


def _log_sigmoid(x):
    return jnp.minimum(x, 0.0) - jnp.log(1.0 + jnp.exp(-jnp.abs(x)))


def _mlstm_kernel(qk_ref, v_ref, bo_ref, bz_ref, gate_ref, gbias_ref, cw_ref, cb_ref, ng_ref,
                  oa_ref, ga_ref, gb_ref, x_ref, wa_ref, wb_ref, wo_ref,
                  o_ref, tail_ref, c_ref, n_ref, m_ref, *, chunk):
    L = chunk
    n_pair = HEADS // 2

    @pl.when(pl.program_id(1) == 0)
    def _():
        tail_ref[...] = jnp.zeros(tail_ref.shape, F32)
        c_ref[...] = jnp.zeros(c_ref.shape, F32)
        n_ref[...] = jnp.zeros(n_ref.shape, F32)
        m_ref[...] = jnp.full(m_ref.shape, NEG_INIT, F32)

    conv = []
    for sl in range(2 * n_pair):
        x = qk_ref[sl].astype(F32)
        tail_ref[sl, SUBLANES:SUBLANES + L, :] = x
        y = x * cw_ref[(CONV_WIDTH - 1) * 8 + sl: (CONV_WIDTH - 1) * 8 + sl + 1, :] + cb_ref[sl:sl + 1, :]
        for d in range(1, CONV_WIDTH):
            j = CONV_WIDTH - 1 - d
            y = y + tail_ref[sl, SUBLANES - d:SUBLANES - d + L, :] * cw_ref[j * 8 + sl: j * 8 + sl + 1, :]
        tail_ref[sl, 0:SUBLANES, :] = x[L - SUBLANES:L]
        conv.append(y * _sigmoid(y))
    lane = lax.broadcasted_iota(jnp.int32, (L, LANES), 1)
    lo_half = lane < QK_DIM

    g = gate_ref[...] + gbias_ref[...]
    ls = _log_sigmoid(g)
    r_i = lax.broadcasted_iota(jnp.int32, (L, L), 0)
    c_i = lax.broadcasted_iota(jnp.int32, (L, L), 1)
    tri = jnp.where(c_i <= r_i, 1.0, 0.0).astype(BF16)
    ls_hi, ls_lo = _split_hi_lo(ls)
    bcum = (jnp.dot(tri, ls_hi, preferred_element_type=F32)
            + jnp.dot(tri, ls_lo, preferred_element_type=F32))
    g_t = jnp.transpose(g)
    b_t = jnp.transpose(bcum)
    causal_t = r_i <= c_i
    lane1 = lax.broadcasted_iota(jnp.int32, (1, LANES), 1)
    nt = (((1,), (1,)), ((), ()))

    hb = [None] * HEADS
    for pr in range(n_pair):
        q_bf = (conv[pr] * (QK_DIM ** -0.5)).astype(BF16)
        k_bf = conv[n_pair + pr].astype(BF16)
        zero = jnp.zeros_like(q_bf)
        q_stack = jnp.concatenate([jnp.where(lo_half, q_bf, zero), jnp.where(lo_half, zero, q_bf)], axis=0)
        k_halves = (jnp.where(lo_half, k_bf, zero), jnp.where(lo_half, zero, k_bf))
        s_pair = lax.dot_general(k_bf, q_stack, nt, preferred_element_type=F32)
        ct_pair = c_ref[pr]
        inter_pair = lax.dot_general(ct_pair.astype(BF16), q_stack, nt, preferred_element_type=F32)
        n_pair_row = n_ref[pr]
        qn_pair = lax.dot_general(jnp.broadcast_to(n_pair_row, (2 * SUBLANES, LANES)).astype(BF16), q_stack, nt,
                                  preferred_element_type=F32)[0:1]

        a_vals, c_upd, n_upd = [], [], []
        for half in range(2):
            h = 2 * pr + half
            cols = slice(half * L, (half + 1) * L)
            ccol = g[:, h: h + 1] - bcum[:, HEADS + h: HEADS + h + 1]
            brow = b_t[HEADS + h: HEADS + h + 1, :]
            irow = g_t[h: h + 1, :]
            blast = brow[:, L - 1: L]
            m_prev = m_ref[h]

            d_t = jnp.where(causal_t, ccol + brow, -jnp.inf)
            inter_log = brow + m_prev
            m_t = jnp.maximum(inter_log, jnp.max(d_t, axis=0, keepdims=True))
            sc = s_pair[:, cols] * jnp.exp(d_t - m_t)
            inter_w = jnp.exp(inter_log - m_t)
            v_t = jnp.transpose(v_ref[h].astype(F32))
            num = (jnp.dot(v_t.astype(BF16), sc.astype(BF16), preferred_element_type=F32)
                   + inter_w * inter_pair[:, cols])
            den = jnp.sum(sc, axis=0, keepdims=True) + inter_w * qn_pair[:, cols]
            h_t = num / jnp.maximum(jnp.abs(den), jnp.exp(-m_t))

            ms = jnp.mean(h_t * h_t, axis=0, keepdims=True)
            y = jnp.transpose(h_t * lax.rsqrt(ms + EPS)) * ng_ref[...]
            bo = bo_ref[h].astype(F32)
            bz = bz_ref[h].astype(F32)
            hb[h] = (_sigmoid(bo) * y * (bz * _sigmoid(bz))).astype(BF16)

            w_log = blast - brow + irow
            m_new = jnp.maximum(blast + m_prev, jnp.max(w_log, axis=-1, keepdims=True))
            a_vals.append(jnp.exp(blast + m_prev - m_new))
            w_row = jnp.exp(w_log - m_new)
            c_upd.append(jnp.dot((v_t * w_row).astype(BF16), k_halves[half], preferred_element_type=F32))
            n_upd.append(jnp.dot(jnp.broadcast_to(w_row, (2 * SUBLANES, L)).astype(BF16), k_halves[half],
                                 preferred_element_type=F32)[0:1])
            m_ref[h] = m_new

        a_row = jnp.where(lane1 < QK_DIM, a_vals[0], a_vals[1])
        c_ref[pr] = a_row * ct_pair + c_upd[0] + c_upd[1]
        n_ref[pr] = a_row * n_pair_row + n_upd[0] + n_upd[1]

    def cat(ref):
        return jnp.concatenate([ref[h] for h in range(HEADS)], axis=1)

    ya = jnp.dot(cat(oa_ref), wa_ref[...], preferred_element_type=F32)
    yb = jnp.dot(jnp.concatenate(hb, axis=1), wb_ref[...], preferred_element_type=F32)
    u = _sigmoid(cat(ga_ref).astype(F32)) * ya + _sigmoid(cat(gb_ref).astype(F32)) * yb
    o_ref[...] = x_ref[...] + jnp.dot(u.astype(BF16), wo_ref[...], preferred_element_type=F32)


def _mlstm_out(p, gates, gate_bias, conv_w, conv_b, norm_g, oa, x2, wa, wb, wo, *, layer, bsz, seq, chunk):
    m = bsz * seq
    nc = seq // chunk

    def slab_map(group):
        return lambda b, c: (group, b * nc + c, 0)

    w_spec = pl.BlockSpec((None, D_MODEL, D_MODEL), lambda b, c: (layer, 0, 0))
    return pl.pallas_call(
        functools.partial(_mlstm_kernel, chunk=chunk),
        grid=(bsz, nc),
        in_specs=[
            pl.BlockSpec((HEADS, chunk, LANES), slab_map(SLAB_BQK // HEADS)),
            pl.BlockSpec((HEADS, chunk, LANES), slab_map(SLAB_BV // HEADS)),
            pl.BlockSpec((HEADS, chunk, LANES), slab_map(SLAB_BO // HEADS)),
            pl.BlockSpec((HEADS, chunk, LANES), slab_map(SLAB_BZ // HEADS)),
            pl.BlockSpec((chunk, LANES), lambda b, c: (b * nc + c, 0)),
            pl.BlockSpec((1, LANES), lambda b, c: (0, 0)),
            pl.BlockSpec((CONV_WIDTH * 8, LANES), lambda b, c: (0, 0)),
            pl.BlockSpec((8, LANES), lambda b, c: (0, 0)),
            pl.BlockSpec((1, V_DIM), lambda b, c: (0, 0)),
            pl.BlockSpec((HEADS, chunk, LANES), slab_map(0)),
            pl.BlockSpec((HEADS, chunk, LANES), slab_map(SLAB_GA // HEADS)),
            pl.BlockSpec((HEADS, chunk, LANES), slab_map(SLAB_GB // HEADS)),
            pl.BlockSpec((chunk, D_MODEL), lambda b, c: (b * nc + c, 0)),
            w_spec, w_spec, w_spec,
        ],
        out_specs=pl.BlockSpec((chunk, D_MODEL), lambda b, c: (b * nc + c, 0)),
        out_shape=jax.ShapeDtypeStruct((m, D_MODEL), F32),
        scratch_shapes=[
            pltpu.VMEM((HEADS, SUBLANES + chunk, LANES), F32),
            pltpu.VMEM((HEADS // 2, V_DIM, LANES), F32),
            pltpu.VMEM((HEADS // 2, 1, LANES), F32),
            pltpu.VMEM((HEADS, 1, 1), F32),
        ],
        compiler_params=pltpu.CompilerParams(
            dimension_semantics=("arbitrary", "arbitrary"), vmem_limit_bytes=VMEM_LIMIT),
        name="mlstm_out",
    )(p, p, p, p, gates, gate_bias, conv_w, conv_b, norm_g, oa, p, p, x2, wa, wb, wo)


def _reorder_w_in(w_in):
    idx = [0]
    for s in _REF_SPLITS:
        idx.append(idx[-1] + s)
    gate0, gate1 = idx[6], idx[8]
    n_qk = idx[2]
    w_in = w_in.astype(BF16)
    w_qk = w_in[:, :, :n_qk].reshape(w_in.shape[:2] + (n_qk // LANES, 2, 2, ROPE_HALF))
    w_qk = jnp.swapaxes(w_qk, 3, 4).reshape(w_in.shape[:2] + (n_qk,))
    w_main = jnp.concatenate([w_qk, w_in[:, :, n_qk:gate0], w_in[:, :, gate1:]], axis=2)
    pad = jnp.zeros(w_in.shape[:2] + (LANES - (gate1 - gate0),), BF16)
    w_gate = jnp.concatenate([w_in[:, :, gate0:gate1], pad], axis=2)
    return w_main, w_gate


def _rope_lanes(per_dim):
    lo, hi = per_dim[:ROPE_HALF], per_dim[ROPE_HALF:]
    return jnp.concatenate([lo, lo, hi, hi])


def _pick(n, pref):
    return pref if n % pref == 0 else n


def kernel(x, positions, norm_g, w_in, q_norm_g, k_norm_g, lambda_qk, attn_norm_g, w_out_a,
           conv_w, conv_b, igate_b, fgate_b, mlstm_norm_g, w_out_b, w_o):
    bsz, seq, _ = x.shape
    m = bsz * seq
    depth = w_in.shape[0]
    tm = _pick(m, 1024)
    blk = _pick(seq, 512)
    chunk = _pick(seq, 256)

    inv_freq = ROPE_THETA ** (-jnp.arange(0, QK_DIM, 2, dtype=F32) / QK_DIM)
    ang = positions.astype(F32).reshape(m, 1) * inv_freq[None, :]
    cos, sin = jnp.cos(ang), jnp.sin(ang)
    cos4 = jnp.concatenate([cos, cos, cos, cos], axis=1)
    sin4 = jnp.concatenate([-sin, -sin, sin, sin], axis=1)
    lanes2 = jnp.arange(2 * LANES)
    grp = 2 * (lanes2 // LANES) + (lanes2 // ROPE_HALF) % 2
    ones_blk = (grp[:, None] == grp[None, :]).astype(BF16)

    x2 = x.reshape(m, D_MODEL)
    w_main, w_gate = _reorder_w_in(w_in)
    wa, wb, wo = w_out_a.astype(BF16), w_out_b.astype(BF16), w_o.astype(BF16)
    for l in range(depth):
        lam_init = 0.8 - 0.6 * math.exp(-0.3 * l)
        p, gates = _inproj(x2, norm_g[l][None, :], w_main, w_gate, layer=l, tm=tm, tn=2560)

        gq = _rope_lanes(q_norm_g[l]) * (QK_DIM ** -0.5 * LOG2_E)
        gk = _rope_lanes(k_norm_g[l])
        gains = jnp.concatenate([gq, gk])[None, :]
        q, k, vt = _attnprep(p, cos4, sin4, gains, ones_blk, tm=_pick(m, 4096), blk=blk)
        oa = _attention(q, k, vt, p, lambda_qk[l], attn_norm_g[l][None, :],
                        bsz=bsz, seq=seq, blk=blk, lam_init=lam_init)

        gate_bias = jnp.concatenate([igate_b[l], fgate_b[l], jnp.zeros((LANES - 2 * HEADS,), F32)])[None, :]
        cw = conv_w[l].reshape(CONV_WIDTH * 8, LANES)
        cb = conv_b[l].reshape(8, LANES)
        x2 = _mlstm_out(p, gates, gate_bias, cw, cb, mlstm_norm_g[l][None, :], oa, x2, wa, wb, wo,
                        layer=l, bsz=bsz, seq=seq, chunk=chunk)
    return x2.reshape(bsz, seq, D_MODEL)
```

```python
import functools
import math

import jax
import jax.numpy as jnp
from jax import lax
from jax.experimental import pallas as pl
from jax.experimental.pallas import tpu as pltpu

F32 = jnp.float32
BF16 = jnp.bfloat16

D_MODEL = 1024
HEADS = 8
QK_DIM = 64
V_DIM = 128
V_EXT = V_DIM + 16
ROPE_HALF = QK_DIM // 2
CONV_WIDTH = 4
ROPE_THETA = 10000.0
EPS = 1e-6
NEG_INIT = -1e30
LOG2_E = math.log2(math.e)
LANES = 128
SUBLANES = 8
VMEM_LIMIT = 48 * 1024 * 1024

SLAB_AQ, SLAB_AK, SLAB_AV, SLAB_AZ = 0, 8, 16, 24
SLAB_BQK, SLAB_BV, SLAB_BO, SLAB_BZ, SLAB_GA, SLAB_GB = 32, 40, 48, 56, 64, 72
N_SLABS = 80
N_MAIN = N_SLABS * LANES

_REF_SPLITS = (1024, 1024, 1024, 1024, 1024, 1024, 8, 8, 1024, 1024, 1024, 1024)


def _split_hi_lo(x):
    hi = x.astype(BF16)
    lo = (x - hi.astype(F32)).astype(BF16)
    return hi, lo


def _sigmoid(x):
    return 1.0 / (1.0 + jnp.exp2(x * (-LOG2_E)))


def _inproj_kernel(x_ref, g_ref, w_ref, wg_ref, p_ref, gate_ref, h_ref, *, n_sub):
    @pl.when(pl.program_id(1) == 0)
    def _():
        x = x_ref[...]
        ms = jnp.mean(x * x, axis=-1, keepdims=True)
        h = ((x * lax.rsqrt(ms + EPS)) * g_ref[...]).astype(BF16)
        h_ref[...] = h
        gate_ref[...] = jnp.dot(h, wg_ref[...], preferred_element_type=F32)

    acc = jnp.dot(h_ref[...], w_ref[...], preferred_element_type=F32)
    for c in range(n_sub):
        p_ref[c] = acc[:, c * LANES:(c + 1) * LANES].astype(BF16)


def _inproj(x2, norm_g, w_main, w_gate, *, layer, tm, tn):
    m = x2.shape[0]
    n_sub = tn // LANES
    return pl.pallas_call(
        functools.partial(_inproj_kernel, n_sub=n_sub),
        grid=(m // tm, N_MAIN // tn),
        in_specs=[
            pl.BlockSpec((tm, D_MODEL), lambda i, j: (i, 0)),
            pl.BlockSpec((1, D_MODEL), lambda i, j: (0, 0)),
            pl.BlockSpec((None, D_MODEL, tn), lambda i, j: (layer, 0, j)),
            pl.BlockSpec((None, D_MODEL, LANES), lambda i, j: (layer, 0, 0)),
        ],
        out_specs=[
            pl.BlockSpec((n_sub, tm, LANES), lambda i, j: (j, i, 0)),
            pl.BlockSpec((tm, LANES), lambda i, j: (i, 0)),
        ],
        out_shape=[
            jax.ShapeDtypeStruct((N_SLABS, m, LANES), BF16),
            jax.ShapeDtypeStruct((m, LANES), F32),
        ],
        scratch_shapes=[pltpu.VMEM((tm, D_MODEL), BF16)],
        compiler_params=pltpu.CompilerParams(
            dimension_semantics=("arbitrary", "arbitrary"), vmem_limit_bytes=VMEM_LIMIT),
        name="inproj",
    )(x2, norm_g, w_main, w_gate)


def _attnprep_kernel(q_ref, k_ref, v_ref, cos_ref, sin_ref, g_ref, ones_ref, qt_ref, ko_ref, vt_ref,
                     *, blk, n_sub):
    x = jnp.concatenate([q_ref[...], k_ref[...]], axis=1).astype(F32)
    hi, lo = _split_hi_lo(x * x)
    ones = ones_ref[...]
    ss = (jnp.dot(hi, ones, preferred_element_type=F32)
          + jnp.dot(lo, ones, preferred_element_type=F32))
    y = (x * lax.rsqrt(ss * (1.0 / QK_DIM) + EPS)) * g_ref[...]
    cos = cos_ref[...]
    sin = sin_ref[...]
    roped = []
    for side in range(2):
        ys = y[:, side * LANES:(side + 1) * LANES]
        roped.append(ys * cos + pltpu.roll(ys, LANES // 2, axis=1) * sin)
    ko_ref[...] = roped[1].astype(BF16)
    for c in range(n_sub):
        rows = slice(c * blk, (c + 1) * blk)
        qt_ref[c] = jnp.transpose(roped[0][rows]).astype(BF16)
        vt_ref[c, 0:V_DIM, :] = jnp.transpose(v_ref[rows, :].astype(F32)).astype(BF16)
        vt_ref[c, V_DIM:V_EXT, :] = jnp.ones((V_EXT - V_DIM, blk), BF16)


def _attnprep(p, cos4, sin4, gains, ones_blk, *, tm, blk):
    m = p.shape[1]
    n_sub = tm // blk
    return pl.pallas_call(
        functools.partial(_attnprep_kernel, blk=blk, n_sub=n_sub),
        grid=(m // tm, HEADS),
        in_specs=[
            pl.BlockSpec((None, tm, LANES), lambda i, h: (SLAB_AQ + h, i, 0)),
            pl.BlockSpec((None, tm, LANES), lambda i, h: (SLAB_AK + h, i, 0)),
            pl.BlockSpec((None, tm, LANES), lambda i, h: (SLAB_AV + h, i, 0)),
            pl.BlockSpec((tm, LANES), lambda i, h: (i, 0)),
            pl.BlockSpec((tm, LANES), lambda i, h: (i, 0)),
            pl.BlockSpec((1, 2 * LANES), lambda i, h: (0, 0)),
            pl.BlockSpec((2 * LANES, 2 * LANES), lambda i, h: (0, 0)),
        ],
        out_specs=[
            pl.BlockSpec((None, n_sub, LANES, blk), lambda i, h: (h, i, 0, 0)),
            pl.BlockSpec((None, tm, LANES), lambda i, h: (h, i, 0)),
            pl.BlockSpec((None, n_sub, V_EXT, blk), lambda i, h: (h, i, 0, 0)),
        ],
        out_shape=[
            jax.ShapeDtypeStruct((HEADS, m // blk, LANES, blk), BF16),
            jax.ShapeDtypeStruct((HEADS, m, LANES), BF16),
            jax.ShapeDtypeStruct((HEADS, m // blk, V_EXT, blk), BF16),
        ],
        compiler_params=pltpu.CompilerParams(
            dimension_semantics=("arbitrary", "arbitrary"), vmem_limit_bytes=VMEM_LIMIT),
        name="attnprep",
    )(p, p, p, cos4, sin4, gains, ones_blk)


def _attn_kernel(q_ref, k_ref, vt_ref, az_ref, lam_ref, g_ref, o_ref,
                 qs_ref, sa_ref, sb_ref, ca_ref, cb_ref, m_ref, acc_ref, *, blk, lam_init):
    nb = q_ref.shape[0]

    def produce(kc, s_ref, c_ref, masked):
        k = k_ref[pl.ds(pl.multiple_of(kc * blk, blk), blk), :]
        s = jnp.dot(k, qs_ref[...], preferred_element_type=F32)
        if masked:
            kpos = lax.broadcasted_iota(jnp.int32, s.shape, 0)
            qpos = lax.broadcasted_iota(jnp.int32, s.shape, 1) & (blk - 1)
            s = jnp.where(kpos <= qpos, s, -jnp.inf)
        s_ref[...] = s
        c_ref[...] = jnp.max(s, axis=0, keepdims=True)

    def consume(kc, s_ref, c_ref):
        m = m_ref[...]
        m_new = jnp.maximum(m, c_ref[...])
        alpha = jnp.exp2(m - m_new)
        p = jnp.exp2(s_ref[...] - m_new)
        pv = jnp.dot(vt_ref[kc], p.astype(BF16), preferred_element_type=F32)
        acc_ref[...] = acc_ref[...] * alpha + pv
        m_ref[...] = m_new

    buf_a = (sa_ref, ca_ref)
    buf_b = (sb_ref, cb_ref)

    def two_chunks(j):
        produce(j + 1, *buf_b, False)
        consume(j, *buf_a)
        produce(j + 2, *buf_a, False)
        consume(j + 1, *buf_b)

    def pair(t, carry):
        two_chunks(2 * t)
        return carry

    def quad(t, carry):
        two_chunks(4 * t)
        two_chunks(4 * t + 2)
        return carry

    lq = lam_ref[...]
    lam = (jnp.exp(jnp.sum(lq[0:1] * lq[1:2], axis=-1, keepdims=True))
           - jnp.exp(jnp.sum(lq[2:3] * lq[3:4], axis=-1, keepdims=True)) + lam_init)

    def setup(qi):
        q_t = q_ref[qi]
        dim = lax.broadcasted_iota(jnp.int32, q_t.shape, 0)
        zero = jnp.zeros_like(q_t)
        map0 = (dim & ROPE_HALF) == 0
        qs_ref[:, 0:blk] = jnp.where(map0, q_t, zero)
        qs_ref[:, blk:2 * blk] = jnp.where(map0, zero, q_t)
        acc_ref[...] = jnp.zeros(acc_ref.shape, F32)
        m_ref[...] = jnp.full(m_ref.shape, -jnp.inf, F32)

    def finalize(qi):
        rows = pl.ds(pl.multiple_of(qi * blk, blk), blk)
        o_t = acc_ref[0:V_DIM, :] / acc_ref[V_DIM:V_DIM + 1, :]
        o = jnp.transpose(o_t[:, 0:blk] - lam * o_t[:, blk:2 * blk])
        ms = jnp.mean(o * o, axis=-1, keepdims=True)
        y = (o * lax.rsqrt(ms + EPS)) * g_ref[...] * (1.0 - lam_init)
        az = az_ref[rows, :].astype(F32)
        o_ref[rows, :] = (y * (az * _sigmoid(az))).astype(BF16)

    setup(0)
    produce(0, *buf_a, True)
    consume(0, *buf_a)

    def qblock(qi, carry):
        finalize(qi - 1)
        setup(qi)
        produce(0, *buf_a, False)

        n_pairs = (qi - 1) // 2
        n_quads = n_pairs // 2
        lax.fori_loop(0, n_quads, quad, 0)
        lax.fori_loop(2 * n_quads, n_pairs, pair, 0)
        rest = qi - 2 * n_pairs

        @pl.when(rest == 1)
        def _():
            produce(qi, *buf_b, True)
            consume(qi - 1, *buf_a)
            consume(qi, *buf_b)

        @pl.when(rest == 2)
        def _():
            produce(qi - 1, *buf_b, False)
            consume(qi - 2, *buf_a)
            produce(qi, *buf_a, True)
            consume(qi - 1, *buf_b)
            consume(qi, *buf_a)

        return carry

    lax.fori_loop(1, nb, qblock, 0)
    finalize(nb - 1)


def _attention(q, k, vt, p, lambda_qk, attn_norm_g, *, bsz, seq, blk, lam_init):
    m = bsz * seq
    nb = seq // blk

    return pl.pallas_call(
        functools.partial(_attn_kernel, blk=blk, lam_init=lam_init),
        grid=(bsz * HEADS,),
        in_specs=[
            pl.BlockSpec((None, nb, LANES, blk), lambda bh: (bh % HEADS, bh // HEADS, 0, 0)),
            pl.BlockSpec((None, seq, LANES), lambda bh: (bh % HEADS, bh // HEADS, 0)),
            pl.BlockSpec((None, nb, V_EXT, blk), lambda bh: (bh % HEADS, bh // HEADS, 0, 0)),
            pl.BlockSpec((None, seq, LANES), lambda bh: (SLAB_AZ + bh % HEADS, bh // HEADS, 0)),
            pl.BlockSpec((4, QK_DIM), lambda bh: (0, 0)),
            pl.BlockSpec((1, V_DIM), lambda bh: (0, 0)),
        ],
        out_specs=pl.BlockSpec((None, seq, LANES), lambda bh: (bh % HEADS, bh // HEADS, 0)),
        out_shape=jax.ShapeDtypeStruct((HEADS, m, LANES), BF16),
        scratch_shapes=[
            pltpu.VMEM((LANES, 2 * blk), BF16),
            pltpu.VMEM((blk, 2 * blk), F32),
            pltpu.VMEM((blk, 2 * blk), F32),
            pltpu.VMEM((1, 2 * blk), F32),
            pltpu.VMEM((1, 2 * blk), F32),
            pltpu.VMEM((1, 2 * blk), F32),
            pltpu.VMEM((V_EXT, 2 * blk), F32),
        ],
        compiler_params=pltpu.CompilerParams(
            dimension_semantics=("arbitrary",), vmem_limit_bytes=VMEM_LIMIT),
        name="diffattn",
    )(q, k, vt, p, lambda_qk, attn_norm_g)


def _log_sigmoid(x):
    return jnp.minimum(x, 0.0) - jnp.log(1.0 + jnp.exp(-jnp.abs(x)))


def _mlstm_kernel(qk_ref, v_ref, bo_ref, bz_ref, gate_ref, gbias_ref, cw_ref, cb_ref, ng_ref,
                  oa_ref, ga_ref, gb_ref, x_ref, wa_ref, wb_ref, wo_ref,
                  o_ref, tail_ref, c_ref, m_ref, *, chunk):
    L = chunk
    n_pair = HEADS // 2

    @pl.when(pl.program_id(1) == 0)
    def _():
        tail_ref[...] = jnp.zeros(tail_ref.shape, F32)
        c_ref[...] = jnp.zeros(c_ref.shape, F32)
        m_ref[...] = jnp.full(m_ref.shape, NEG_INIT, F32)

    conv = []
    for sl in range(2 * n_pair):
        x = qk_ref[sl].astype(F32)
        tail_ref[sl, SUBLANES:SUBLANES + L, :] = x
        y = x * cw_ref[(CONV_WIDTH - 1) * 8 + sl: (CONV_WIDTH - 1) * 8 + sl + 1, :] + cb_ref[sl:sl + 1, :]
        for d in range(1, CONV_WIDTH):
            j = CONV_WIDTH - 1 - d
            y = y + tail_ref[sl, SUBLANES - d:SUBLANES - d + L, :] * cw_ref[j * 8 + sl: j * 8 + sl + 1, :]
        tail_ref[sl, 0:SUBLANES, :] = x[L - SUBLANES:L]
        conv.append(y * _sigmoid(y))
    lane = lax.broadcasted_iota(jnp.int32, (L, LANES), 1)
    lo_half = lane < QK_DIM

    g = gate_ref[...] + gbias_ref[...]
    ls = _log_sigmoid(g)
    r_i = lax.broadcasted_iota(jnp.int32, (L, L), 0)
    c_i = lax.broadcasted_iota(jnp.int32, (L, L), 1)
    tri = jnp.where(c_i <= r_i, 1.0, 0.0).astype(BF16)
    ls_hi, ls_lo = _split_hi_lo(ls)
    bcum = (jnp.dot(tri, ls_hi, preferred_element_type=F32)
            + jnp.dot(tri, ls_lo, preferred_element_type=F32))
    g_t = jnp.transpose(g)
    b_t = jnp.transpose(bcum)
    causal_t = r_i <= c_i
    lane1 = lax.broadcasted_iota(jnp.int32, (1, LANES), 1)
    nt = (((1,), (1,)), ((), ()))
    ones_rows = jnp.ones((V_EXT - V_DIM, L), F32)

    hb = [None] * HEADS
    for pr in range(n_pair):
        q_bf = (conv[pr] * (QK_DIM ** -0.5)).astype(BF16)
        k_bf = conv[n_pair + pr].astype(BF16)
        zero = jnp.zeros_like(q_bf)
        q_stack = jnp.concatenate([jnp.where(lo_half, q_bf, zero), jnp.where(lo_half, zero, q_bf)], axis=0)
        k_halves = (jnp.where(lo_half, k_bf, zero), jnp.where(lo_half, zero, k_bf))
        s_pair = lax.dot_general(k_bf, q_stack, nt, preferred_element_type=F32)
        ct_pair = c_ref[pr]
        inter_pair = lax.dot_general(ct_pair.astype(BF16), q_stack, nt, preferred_element_type=F32)

        a_vals, c_upd = [], []
        for half in range(2):
            h = 2 * pr + half
            cols = slice(half * L, (half + 1) * L)
            ccol = g[:, h: h + 1] - bcum[:, HEADS + h: HEADS + h + 1]
            brow = b_t[HEADS + h: HEADS + h + 1, :]
            irow = g_t[h: h + 1, :]
            blast = brow[:, L - 1: L]
            m_prev = m_ref[h]

            d_t = jnp.where(causal_t, ccol + brow, -jnp.inf)
            inter_log = brow + m_prev
            m_t = jnp.maximum(inter_log, jnp.max(d_t, axis=0, keepdims=True))
            sc = s_pair[:, cols] * jnp.exp(d_t - m_t)
            inter_w = jnp.exp(inter_log - m_t)
            v_t = jnp.concatenate([jnp.transpose(v_ref[h].astype(F32)), ones_rows], axis=0)
            nd = (jnp.dot(v_t.astype(BF16), sc.astype(BF16), preferred_element_type=F32)
                  + inter_w * inter_pair[:, cols])
            h_t = nd[0:V_DIM] / jnp.maximum(jnp.abs(nd[V_DIM:V_DIM + 1]), jnp.exp(-m_t))

            ms = jnp.mean(h_t * h_t, axis=0, keepdims=True)
            y = jnp.transpose(h_t * lax.rsqrt(ms + EPS)) * ng_ref[...]
            bo = bo_ref[h].astype(F32)
            bz = bz_ref[h].astype(F32)
            hb[h] = (_sigmoid(bo) * y * (bz * _sigmoid(bz))).astype(BF16)

            w_log = blast - brow + irow
            m_new = jnp.maximum(blast + m_prev, jnp.max(w_log, axis=-1, keepdims=True))
            a_vals.append(jnp.exp(blast + m_prev - m_new))
            w_row = jnp.exp(w_log - m_new)
            c_upd.append(jnp.dot((v_t * w_row).astype(BF16), k_halves[half], preferred_element_type=F32))
            m_ref[h] = m_new

        a_row = jnp.where(lane1 < QK_DIM, a_vals[0], a_vals[1])
        c_ref[pr] = a_row * ct_pair + c_upd[0] + c_upd[1]

    def cat(ref):
        return jnp.concatenate([ref[h] for h in range(HEADS)], axis=1)

    ya = jnp.dot(cat(oa_ref), wa_ref[...], preferred_element_type=F32)
    yb = jnp.dot(jnp.concatenate(hb, axis=1), wb_ref[...], preferred_element_type=F32)
    u = _sigmoid(cat(ga_ref).astype(F32)) * ya + _sigmoid(cat(gb_ref).astype(F32)) * yb
    o_ref[...] = x_ref[...] + jnp.dot(u.astype(BF16), wo_ref[...], preferred_element_type=F32)


def _mlstm_out(p, gates, gate_bias, conv_w, conv_b, norm_g, oa, x2, wa, wb, wo, *, layer, bsz, seq, chunk):
    m = bsz * seq
    nc = seq // chunk

    def slab_map(group):
        return lambda b, c: (group, b * nc + c, 0)

    w_spec = pl.BlockSpec((None, D_MODEL, D_MODEL), lambda b, c: (layer, 0, 0))
    return pl.pallas_call(
        functools.partial(_mlstm_kernel, chunk=chunk),
        grid=(bsz, nc),
        in_specs=[
            pl.BlockSpec((HEADS, chunk, LANES), slab_map(SLAB_BQK // HEADS)),
            pl.BlockSpec((HEADS, chunk, LANES), slab_map(SLAB_BV // HEADS)),
            pl.BlockSpec((HEADS, chunk, LANES), slab_map(SLAB_BO // HEADS)),
            pl.BlockSpec((HEADS, chunk, LANES), slab_map(SLAB_BZ // HEADS)),
            pl.BlockSpec((chunk, LANES), lambda b, c: (b * nc + c, 0)),
            pl.BlockSpec((1, LANES), lambda b, c: (0, 0)),
            pl.BlockSpec((CONV_WIDTH * 8, LANES), lambda b, c: (0, 0)),
            pl.BlockSpec((8, LANES), lambda b, c: (0, 0)),
            pl.BlockSpec((1, V_DIM), lambda b, c: (0, 0)),
            pl.BlockSpec((HEADS, chunk, LANES), slab_map(0)),
            pl.BlockSpec((HEADS, chunk, LANES), slab_map(SLAB_GA // HEADS)),
            pl.BlockSpec((HEADS, chunk, LANES), slab_map(SLAB_GB // HEADS)),
            pl.BlockSpec((chunk, D_MODEL), lambda b, c: (b * nc + c, 0)),
            w_spec, w_spec, w_spec,
        ],
        out_specs=pl.BlockSpec((chunk, D_MODEL), lambda b, c: (b * nc + c, 0)),
        out_shape=jax.ShapeDtypeStruct((m, D_MODEL), F32),
        scratch_shapes=[
            pltpu.VMEM((HEADS, SUBLANES + chunk, LANES), F32),
            pltpu.VMEM((HEADS // 2, V_EXT, LANES), F32),
            pltpu.VMEM((HEADS, 1, 1), F32),
        ],
        compiler_params=pltpu.CompilerParams(
            dimension_semantics=("arbitrary", "arbitrary"), vmem_limit_bytes=VMEM_LIMIT),
        name="mlstm_out",
    )(p, p, p, p, gates, gate_bias, conv_w, conv_b, norm_g, oa, p, p, x2, wa, wb, wo)


def _reorder_w_in(w_in):
    idx = [0]
    for s in _REF_SPLITS:
        idx.append(idx[-1] + s)
    gate0, gate1 = idx[6], idx[8]
    n_qk = idx[2]
    w_in = w_in.astype(BF16)
    w_qk = w_in[:, :, :n_qk].reshape(w_in.shape[:2] + (n_qk // LANES, 2, 2, ROPE_HALF))
    w_qk = jnp.swapaxes(w_qk, 3, 4).reshape(w_in.shape[:2] + (n_qk,))
    w_main = jnp.concatenate([w_qk, w_in[:, :, n_qk:gate0], w_in[:, :, gate1:]], axis=2)
    pad = jnp.zeros(w_in.shape[:2] + (LANES - (gate1 - gate0),), BF16)
    w_gate = jnp.concatenate([w_in[:, :, gate0:gate1], pad], axis=2)
    return w_main, w_gate


def _rope_lanes(per_dim):
    lo, hi = per_dim[:ROPE_HALF], per_dim[ROPE_HALF:]
    return jnp.concatenate([lo, lo, hi, hi])


def _pick(n, pref):
    return pref if n % pref == 0 else n


def kernel(x, positions, norm_g, w_in, q_norm_g, k_norm_g, lambda_qk, attn_norm_g, w_out_a,
           conv_w, conv_b, igate_b, fgate_b, mlstm_norm_g, w_out_b, w_o):
    bsz, seq, _ = x.shape
    m = bsz * seq
    depth = w_in.shape[0]
    tm = _pick(m, 1024)
    blk = _pick(seq, 512)
    chunk = _pick(seq, 256)

    inv_freq = ROPE_THETA ** (-jnp.arange(0, QK_DIM, 2, dtype=F32) / QK_DIM)
    ang = positions.astype(F32).reshape(m, 1) * inv_freq[None, :]
    cos, sin = jnp.cos(ang), jnp.sin(ang)
    cos4 = jnp.concatenate([cos, cos, cos, cos], axis=1)
    sin4 = jnp.concatenate([-sin, -sin, sin, sin], axis=1)
    lanes2 = jnp.arange(2 * LANES)
    grp = 2 * (lanes2 // LANES) + (lanes2 // ROPE_HALF) % 2
    ones_blk = (grp[:, None] == grp[None, :]).astype(BF16)

    x2 = x.reshape(m, D_MODEL)
    w_main, w_gate = _reorder_w_in(w_in)
    wa, wb, wo = w_out_a.astype(BF16), w_out_b.astype(BF16), w_o.astype(BF16)
    for l in range(depth):
        lam_init = 0.8 - 0.6 * math.exp(-0.3 * l)
        p, gates = _inproj(x2, norm_g[l][None, :], w_main, w_gate, layer=l, tm=tm, tn=2560)

        gq = _rope_lanes(q_norm_g[l]) * (QK_DIM ** -0.5 * LOG2_E)
        gk = _rope_lanes(k_norm_g[l])
        gains = jnp.concatenate([gq, gk])[None, :]
        q, k, vt = _attnprep(p, cos4, sin4, gains, ones_blk, tm=_pick(m, 4096), blk=blk)
        oa = _attention(q, k, vt, p, lambda_qk[l], attn_norm_g[l][None, :],
                        bsz=bsz, seq=seq, blk=blk, lam_init=lam_init)

        gate_bias = jnp.concatenate([igate_b[l], fgate_b[l], jnp.zeros((LANES - 2 * HEADS,), F32)])[None, :]
        cw = conv_w[l].reshape(CONV_WIDTH * 8, LANES)
        cb = conv_b[l].reshape(8, LANES)
        x2 = _mlstm_out(p, gates, gate_bias, cw, cb, mlstm_norm_g[l][None, :], oa, x2, wa, wb, wo,
                        layer=l, bsz=bsz, seq=seq, chunk=chunk)
    return x2.reshape(bsz, seq, D_MODEL)
```

```python
import functools
import math

import jax
import jax.numpy as jnp
from jax import lax
from jax.experimental import pallas as pl
from jax.experimental.pallas import tpu as pltpu

F32 = jnp.float32
BF16 = jnp.bfloat16

D_MODEL = 1024
HEADS = 8
QK_DIM = 64
V_DIM = 128
V_EXT = V_DIM + 16
ROPE_HALF = QK_DIM // 2
CONV_WIDTH = 4
ROPE_THETA = 10000.0
EPS = 1e-6
NEG_INIT = -1e30
LOG2_E = math.log2(math.e)
LANES = 128
SUBLANES = 8
VMEM_LIMIT = 48 * 1024 * 1024

SLAB_AQ, SLAB_AK, SLAB_AV, SLAB_AZ = 0, 8, 16, 24
SLAB_BQK, SLAB_BV, SLAB_BO, SLAB_BZ, SLAB_GA, SLAB_GB = 32, 40, 48, 56, 64, 72
N_SLABS = 80
N_MAIN = N_SLABS * LANES

_REF_SPLITS = (1024, 1024, 1024, 1024, 1024, 1024, 8, 8, 1024, 1024, 1024, 1024)


def _split_hi_lo(x):
    hi = x.astype(BF16)
    lo = (x - hi.astype(F32)).astype(BF16)
    return hi, lo


def _sigmoid(x):
    return 1.0 / (1.0 + jnp.exp2(x * (-LOG2_E)))


def _inproj_kernel(x_ref, g_ref, w_ref, wg_ref, p_ref, gate_ref, h_ref, *, n_sub):
    @pl.when(pl.program_id(1) == 0)
    def _():
        x = x_ref[...]
        ms = jnp.mean(x * x, axis=-1, keepdims=True)
        h = ((x * lax.rsqrt(ms + EPS)) * g_ref[...]).astype(BF16)
        h_ref[...] = h
        gate_ref[...] = jnp.dot(h, wg_ref[...], preferred_element_type=F32)

    acc = jnp.dot(h_ref[...], w_ref[...], preferred_element_type=F32)
    for c in range(n_sub):
        p_ref[c] = acc[:, c * LANES:(c + 1) * LANES].astype(BF16)


def _inproj(x2, norm_g, w_main, w_gate, *, layer, tm, tn):
    m = x2.shape[0]
    n_sub = tn // LANES
    return pl.pallas_call(
        functools.partial(_inproj_kernel, n_sub=n_sub),
        grid=(m // tm, N_MAIN // tn),
        in_specs=[
            pl.BlockSpec((tm, D_MODEL), lambda i, j: (i, 0)),
            pl.BlockSpec((1, D_MODEL), lambda i, j: (0, 0)),
            pl.BlockSpec((None, D_MODEL, tn), lambda i, j: (layer, 0, j)),
            pl.BlockSpec((None, D_MODEL, LANES), lambda i, j: (layer, 0, 0)),
        ],
        out_specs=[
            pl.BlockSpec((n_sub, tm, LANES), lambda i, j: (j, i, 0)),
            pl.BlockSpec((tm, LANES), lambda i, j: (i, 0)),
        ],
        out_shape=[
            jax.ShapeDtypeStruct((N_SLABS, m, LANES), BF16),
            jax.ShapeDtypeStruct((m, LANES), F32),
        ],
        scratch_shapes=[pltpu.VMEM((tm, D_MODEL), BF16)],
        compiler_params=pltpu.CompilerParams(
            dimension_semantics=("arbitrary", "arbitrary"), vmem_limit_bytes=VMEM_LIMIT),
        name="inproj",
    )(x2, norm_g, w_main, w_gate)


def _attnprep_kernel(q_ref, k_ref, v_ref, cos_ref, sin_ref, g_ref, ones_ref, qt_ref, ko_ref, vt_ref,
                     *, blk, n_sub):
    x = jnp.concatenate([q_ref[...], k_ref[...]], axis=1).astype(F32)
    hi, lo = _split_hi_lo(x * x)
    ones = ones_ref[...]
    ss = (jnp.dot(hi, ones, preferred_element_type=F32)
          + jnp.dot(lo, ones, preferred_element_type=F32))
    y = (x * lax.rsqrt(ss * (1.0 / QK_DIM) + EPS)) * g_ref[...]
    cos = cos_ref[...]
    sin = sin_ref[...]
    roped = []
    for side in range(2):
        ys = y[:, side * LANES:(side + 1) * LANES]
        roped.append(ys * cos + pltpu.roll(ys, LANES // 2, axis=1) * sin)
    ko_ref[...] = roped[1].astype(BF16)
    for c in range(n_sub):
        rows = slice(c * blk, (c + 1) * blk)
        qt_ref[c] = jnp.transpose(roped[0][rows]).astype(BF16)
        vt_ref[c, 0:V_DIM, :] = jnp.transpose(v_ref[rows, :].astype(F32)).astype(BF16)
        vt_ref[c, V_DIM:V_EXT, :] = jnp.ones((V_EXT - V_DIM, blk), BF16)


def _attnprep(p, cos4, sin4, gains, ones_blk, *, tm, blk):
    m = p.shape[1]
    n_sub = tm // blk
    return pl.pallas_call(
        functools.partial(_attnprep_kernel, blk=blk, n_sub=n_sub),
        grid=(m // tm, HEADS),
        in_specs=[
            pl.BlockSpec((None, tm, LANES), lambda i, h: (SLAB_AQ + h, i, 0)),
            pl.BlockSpec((None, tm, LANES), lambda i, h: (SLAB_AK + h, i, 0)),
            pl.BlockSpec((None, tm, LANES), lambda i, h: (SLAB_AV + h, i, 0)),
            pl.BlockSpec((tm, LANES), lambda i, h: (i, 0)),
            pl.BlockSpec((tm, LANES), lambda i, h: (i, 0)),
            pl.BlockSpec((1, 2 * LANES), lambda i, h: (0, 0)),
            pl.BlockSpec((2 * LANES, 2 * LANES), lambda i, h: (0, 0)),
        ],
        out_specs=[
            pl.BlockSpec((None, n_sub, LANES, blk), lambda i, h: (h, i, 0, 0)),
            pl.BlockSpec((None, tm, LANES), lambda i, h: (h, i, 0)),
            pl.BlockSpec((None, n_sub, V_EXT, blk), lambda i, h: (h, i, 0, 0)),
        ],
        out_shape=[
            jax.ShapeDtypeStruct((HEADS, m // blk, LANES, blk), BF16),
            jax.ShapeDtypeStruct((HEADS, m, LANES), BF16),
            jax.ShapeDtypeStruct((HEADS, m // blk, V_EXT, blk), BF16),
        ],
        compiler_params=pltpu.CompilerParams(
            dimension_semantics=("arbitrary", "arbitrary"), vmem_limit_bytes=VMEM_LIMIT),
        name="attnprep",
    )(p, p, p, cos4, sin4, gains, ones_blk)


def _attn_kernel(q_ref, k_ref, vt_ref, az_ref, lam_ref, g_ref, o_ref,
                 qs_ref, sa_ref, sb_ref, ca_ref, cb_ref, m_ref, acc_ref, *, blk, lam_init):
    nb = q_ref.shape[0]

    def produce(kc, s_ref, c_ref, masked):
        k = k_ref[pl.ds(pl.multiple_of(kc * blk, blk), blk), :]
        s = jnp.dot(k, qs_ref[...], preferred_element_type=F32)
        if masked:
            kpos = lax.broadcasted_iota(jnp.int32, s.shape, 0)
            qpos = lax.broadcasted_iota(jnp.int32, s.shape, 1) & (blk - 1)
            s = jnp.where(kpos <= qpos, s, -jnp.inf)
        s_ref[...] = s
        c_ref[...] = jnp.max(s, axis=0, keepdims=True)

    def consume(kc, s_ref, c_ref):
        m = m_ref[...]
        m_new = jnp.maximum(m, c_ref[...])
        alpha = jnp.exp2(m - m_new)
        p = jnp.exp2(s_ref[...] - m_new)
        pv = jnp.dot(vt_ref[kc], p.astype(BF16), preferred_element_type=F32)
        acc_ref[...] = acc_ref[...] * alpha + pv
        m_ref[...] = m_new

    buf_a = (sa_ref, ca_ref)
    buf_b = (sb_ref, cb_ref)

    def two_chunks(j):
        produce(j + 1, *buf_b, False)
        consume(j, *buf_a)
        produce(j + 2, *buf_a, False)
        consume(j + 1, *buf_b)

    def pair(t, carry):
        two_chunks(2 * t)
        return carry

    def quad(t, carry):
        two_chunks(4 * t)
        two_chunks(4 * t + 2)
        return carry

    def octet(t, carry):
        for c in range(0, 8, 2):
            two_chunks(8 * t + c)
        return carry

    lq = lam_ref[...]
    lam = (jnp.exp(jnp.sum(lq[0:1] * lq[1:2], axis=-1, keepdims=True))
           - jnp.exp(jnp.sum(lq[2:3] * lq[3:4], axis=-1, keepdims=True)) + lam_init)

    def setup(qi):
        q_t = q_ref[qi]
        dim = lax.broadcasted_iota(jnp.int32, q_t.shape, 0)
        zero = jnp.zeros_like(q_t)
        map0 = (dim & ROPE_HALF) == 0
        qs_ref[:, 0:blk] = jnp.where(map0, q_t, zero)
        qs_ref[:, blk:2 * blk] = jnp.where(map0, zero, q_t)
        acc_ref[...] = jnp.zeros(acc_ref.shape, F32)
        m_ref[...] = jnp.full(m_ref.shape, -jnp.inf, F32)

    def finalize(qi):
        rows = pl.ds(pl.multiple_of(qi * blk, blk), blk)
        o_t = acc_ref[0:V_DIM, :] / acc_ref[V_DIM:V_DIM + 1, :]
        o = jnp.transpose(o_t[:, 0:blk] - lam * o_t[:, blk:2 * blk])
        ms = jnp.mean(o * o, axis=-1, keepdims=True)
        y = (o * lax.rsqrt(ms + EPS)) * g_ref[...] * (1.0 - lam_init)
        az = az_ref[rows, :].astype(F32)
        o_ref[rows, :] = (y * (az * _sigmoid(az))).astype(BF16)

    setup(0)
    produce(0, *buf_a, True)
    consume(0, *buf_a)

    def qblock(qi, carry):
        finalize(qi - 1)
        setup(qi)
        produce(0, *buf_a, False)

        n_pairs = (qi - 1) // 2
        n_quads = n_pairs // 2
        n_octets = n_quads // 2
        lax.fori_loop(0, n_octets, octet, 0)
        lax.fori_loop(2 * n_octets, n_quads, quad, 0)
        lax.fori_loop(2 * n_quads, n_pairs, pair, 0)
        rest = qi - 2 * n_pairs

        @pl.when(rest == 1)
        def _():
            produce(qi, *buf_b, True)
            consume(qi - 1, *buf_a)
            consume(qi, *buf_b)

        @pl.when(rest == 2)
        def _():
            produce(qi - 1, *buf_b, False)
            consume(qi - 2, *buf_a)
            produce(qi, *buf_a, True)
            consume(qi - 1, *buf_b)
            consume(qi, *buf_a)

        return carry

    lax.fori_loop(1, nb, qblock, 0)
    finalize(nb - 1)


def _attention(q, k, vt, p, lambda_qk, attn_norm_g, *, bsz, seq, blk, lam_init):
    m = bsz * seq
    nb = seq // blk

    return pl.pallas_call(
        functools.partial(_attn_kernel, blk=blk, lam_init=lam_init),
        grid=(bsz * HEADS,),
        in_specs=[
            pl.BlockSpec((None, nb, LANES, blk), lambda bh: (bh % HEADS, bh // HEADS, 0, 0)),
            pl.BlockSpec((None, seq, LANES), lambda bh: (bh % HEADS, bh // HEADS, 0)),
            pl.BlockSpec((None, nb, V_EXT, blk), lambda bh: (bh % HEADS, bh // HEADS, 0, 0)),
            pl.BlockSpec((None, seq, LANES), lambda bh: (SLAB_AZ + bh % HEADS, bh // HEADS, 0)),
            pl.BlockSpec((4, QK_DIM), lambda bh: (0, 0)),
            pl.BlockSpec((1, V_DIM), lambda bh: (0, 0)),
        ],
        out_specs=pl.BlockSpec((None, seq, LANES), lambda bh: (bh % HEADS, bh // HEADS, 0)),
        out_shape=jax.ShapeDtypeStruct((HEADS, m, LANES), BF16),
        scratch_shapes=[
            pltpu.VMEM((LANES, 2 * blk), BF16),
            pltpu.VMEM((blk, 2 * blk), F32),
            pltpu.VMEM((blk, 2 * blk), F32),
            pltpu.VMEM((1, 2 * blk), F32),
            pltpu.VMEM((1, 2 * blk), F32),
            pltpu.VMEM((1, 2 * blk), F32),
            pltpu.VMEM((V_EXT, 2 * blk), F32),
        ],
        compiler_params=pltpu.CompilerParams(
            dimension_semantics=("arbitrary",), vmem_limit_bytes=VMEM_LIMIT),
        name="diffattn",
    )(q, k, vt, p, lambda_qk, attn_norm_g)


def _log_sigmoid(x):
    return jnp.minimum(x, 0.0) - jnp.log(1.0 + jnp.exp(-jnp.abs(x)))


def _mlstm_kernel(qk_ref, v_ref, bo_ref, bz_ref, gate_ref, gbias_ref, cw_ref, cb_ref, ng_ref,
                  oa_ref, ga_ref, gb_ref, x_ref, wa_ref, wb_ref, wo_ref,
                  o_ref, tail_ref, c_ref, m_ref, *, chunk):
    L = chunk
    n_pair = HEADS // 2

    @pl.when(pl.program_id(1) == 0)
    def _():
        tail_ref[...] = jnp.zeros(tail_ref.shape, F32)
        c_ref[...] = jnp.zeros(c_ref.shape, F32)
        m_ref[...] = jnp.full(m_ref.shape, NEG_INIT, F32)

    conv = []
    for sl in range(2 * n_pair):
        x = qk_ref[sl].astype(F32)
        tail_ref[sl, SUBLANES:SUBLANES + L, :] = x
        y = x * cw_ref[(CONV_WIDTH - 1) * 8 + sl: (CONV_WIDTH - 1) * 8 + sl + 1, :] + cb_ref[sl:sl + 1, :]
        for d in range(1, CONV_WIDTH):
            j = CONV_WIDTH - 1 - d
            y = y + tail_ref[sl, SUBLANES - d:SUBLANES - d + L, :] * cw_ref[j * 8 + sl: j * 8 + sl + 1, :]
        tail_ref[sl, 0:SUBLANES, :] = x[L - SUBLANES:L]
        conv.append(y * _sigmoid(y))
    lane = lax.broadcasted_iota(jnp.int32, (L, LANES), 1)
    lo_half = lane < QK_DIM

    g = gate_ref[...] + gbias_ref[...]
    ls = _log_sigmoid(g)
    r_i = lax.broadcasted_iota(jnp.int32, (L, L), 0)
    c_i = lax.broadcasted_iota(jnp.int32, (L, L), 1)
    tri = jnp.where(c_i <= r_i, 1.0, 0.0).astype(BF16)
    ls_hi, ls_lo = _split_hi_lo(ls)
    bcum = (jnp.dot(tri, ls_hi, preferred_element_type=F32)
            + jnp.dot(tri, ls_lo, preferred_element_type=F32))
    g_t = jnp.transpose(g)
    b_t = jnp.transpose(bcum)
    causal_t = r_i <= c_i
    lane1 = lax.broadcasted_iota(jnp.int32, (1, LANES), 1)
    nt = (((1,), (1,)), ((), ()))
    ones_rows = jnp.ones((V_EXT - V_DIM, L), F32)

    hb = [None] * HEADS
    for pr in range(n_pair):
        q_bf = (conv[pr] * (QK_DIM ** -0.5)).astype(BF16)
        k_bf = conv[n_pair + pr].astype(BF16)
        zero = jnp.zeros_like(q_bf)
        q_stack = jnp.concatenate([jnp.where(lo_half, q_bf, zero), jnp.where(lo_half, zero, q_bf)], axis=0)
        k_halves = (jnp.where(lo_half, k_bf, zero), jnp.where(lo_half, zero, k_bf))
        s_pair = lax.dot_general(k_bf, q_stack, nt, preferred_element_type=F32)
        ct_pair = c_ref[pr]
        inter_pair = lax.dot_general(ct_pair.astype(BF16), q_stack, nt, preferred_element_type=F32)

        a_vals, c_upd = [], []
        for half in range(2):
            h = 2 * pr + half
            cols = slice(half * L, (half + 1) * L)
            ccol = g[:, h: h + 1] - bcum[:, HEADS + h: HEADS + h + 1]
            brow = b_t[HEADS + h: HEADS + h + 1, :]
            irow = g_t[h: h + 1, :]
            blast = brow[:, L - 1: L]
            m_prev = m_ref[h]

            d_t = jnp.where(causal_t, ccol + brow, -jnp.inf)
            inter_log = brow + m_prev
            m_t = jnp.maximum(inter_log, jnp.max(d_t, axis=0, keepdims=True))
            sc = s_pair[:, cols] * jnp.exp(d_t - m_t)
            inter_w = jnp.exp(inter_log - m_t)
            v_t = jnp.concatenate([jnp.transpose(v_ref[h].astype(F32)), ones_rows], axis=0)
            nd = (jnp.dot(v_t.astype(BF16), sc.astype(BF16), preferred_element_type=F32)
                  + inter_w * inter_pair[:, cols])
            h_t = nd[0:V_DIM] / jnp.maximum(jnp.abs(nd[V_DIM:V_DIM + 1]), jnp.exp(-m_t))

            ms = jnp.mean(h_t * h_t, axis=0, keepdims=True)
            y = jnp.transpose(h_t * lax.rsqrt(ms + EPS)) * ng_ref[...]
            bo = bo_ref[h].astype(F32)
            bz = bz_ref[h].astype(F32)
            hb[h] = (_sigmoid(bo) * y * (bz * _sigmoid(bz))).astype(BF16)

            w_log = blast - brow + irow
            m_new = jnp.maximum(blast + m_prev, jnp.max(w_log, axis=-1, keepdims=True))
            a_vals.append(jnp.exp(blast + m_prev - m_new))
            w_row = jnp.exp(w_log - m_new)
            c_upd.append(jnp.dot((v_t * w_row).astype(BF16), k_halves[half], preferred_element_type=F32))
            m_ref[h] = m_new

        a_row = jnp.where(lane1 < QK_DIM, a_vals[0], a_vals[1])
        c_ref[pr] = a_row * ct_pair + c_upd[0] + c_upd[1]

    def cat(ref):
        return jnp.concatenate([ref[h] for h in range(HEADS)], axis=1)

    ya = jnp.dot(cat(oa_ref), wa_ref[...], preferred_element_type=F32)
    yb = jnp.dot(jnp.concatenate(hb, axis=1), wb_ref[...], preferred_element_type=F32)
    u = _sigmoid(cat(ga_ref).astype(F32)) * ya + _sigmoid(cat(gb_ref).astype(F32)) * yb
    o_ref[...] = x_ref[...] + jnp.dot(u.astype(BF16), wo_ref[...], preferred_element_type=F32)


def _mlstm_out(p, gates, gate_bias, conv_w, conv_b, norm_g, oa, x2, wa, wb, wo, *, layer, bsz, seq, chunk):
    m = bsz * seq
    nc = seq // chunk

    def slab_map(group):
        return lambda b, c: (group, b * nc + c, 0)

    w_spec = pl.BlockSpec((None, D_MODEL, D_MODEL), lambda b, c: (layer, 0, 0))
    return pl.pallas_call(
        functools.partial(_mlstm_kernel, chunk=chunk),
        grid=(bsz, nc),
        in_specs=[
            pl.BlockSpec((HEADS, chunk, LANES), slab_map(SLAB_BQK // HEADS)),
            pl.BlockSpec((HEADS, chunk, LANES), slab_map(SLAB_BV // HEADS)),
            pl.BlockSpec((HEADS, chunk, LANES), slab_map(SLAB_BO // HEADS)),
            pl.BlockSpec((HEADS, chunk, LANES), slab_map(SLAB_BZ // HEADS)),
            pl.BlockSpec((chunk, LANES), lambda b, c: (b * nc + c, 0)),
            pl.BlockSpec((1, LANES), lambda b, c: (0, 0)),
            pl.BlockSpec((CONV_WIDTH * 8, LANES), lambda b, c: (0, 0)),
            pl.BlockSpec((8, LANES), lambda b, c: (0, 0)),
            pl.BlockSpec((1, V_DIM), lambda b, c: (0, 0)),
            pl.BlockSpec((HEADS, chunk, LANES), slab_map(0)),
            pl.BlockSpec((HEADS, chunk, LANES), slab_map(SLAB_GA // HEADS)),
            pl.BlockSpec((HEADS, chunk, LANES), slab_map(SLAB_GB // HEADS)),
            pl.BlockSpec((chunk, D_MODEL), lambda b, c: (b * nc + c, 0)),
            w_spec, w_spec, w_spec,
        ],
        out_specs=pl.BlockSpec((chunk, D_MODEL), lambda b, c: (b * nc + c, 0)),
        out_shape=jax.ShapeDtypeStruct((m, D_MODEL), F32),
        scratch_shapes=[
            pltpu.VMEM((HEADS, SUBLANES + chunk, LANES), F32),
            pltpu.VMEM((HEADS // 2, V_EXT, LANES), F32),
            pltpu.VMEM((HEADS, 1, 1), F32),
        ],
        compiler_params=pltpu.CompilerParams(
            dimension_semantics=("arbitrary", "arbitrary"), vmem_limit_bytes=VMEM_LIMIT),
        name="mlstm_out",
    )(p, p, p, p, gates, gate_bias, conv_w, conv_b, norm_g, oa, p, p, x2, wa, wb, wo)


def _reorder_w_in(w_in):
    idx = [0]
    for s in _REF_SPLITS:
        idx.append(idx[-1] + s)
    gate0, gate1 = idx[6], idx[8]
    n_qk = idx[2]
    w_in = w_in.astype(BF16)
    w_qk = w_in[:, :, :n_qk].reshape(w_in.shape[:2] + (n_qk // LANES, 2, 2, ROPE_HALF))
    w_qk = jnp.swapaxes(w_qk, 3, 4).reshape(w_in.shape[:2] + (n_qk,))
    w_main = jnp.concatenate([w_qk, w_in[:, :, n_qk:gate0], w_in[:, :, gate1:]], axis=2)
    pad = jnp.zeros(w_in.shape[:2] + (LANES - (gate1 - gate0),), BF16)
    w_gate = jnp.concatenate([w_in[:, :, gate0:gate1], pad], axis=2)
    return w_main, w_gate


def _rope_lanes(per_dim):
    lo, hi = per_dim[:ROPE_HALF], per_dim[ROPE_HALF:]
    return jnp.concatenate([lo, lo, hi, hi])


def _pick(n, pref):
    return pref if n % pref == 0 else n


def kernel(x, positions, norm_g, w_in, q_norm_g, k_norm_g, lambda_qk, attn_norm_g, w_out_a,
           conv_w, conv_b, igate_b, fgate_b, mlstm_norm_g, w_out_b, w_o):
    bsz, seq, _ = x.shape
    m = bsz * seq
    depth = w_in.shape[0]
    tm = _pick(m, 1024)
    blk = _pick(seq, 512)
    chunk = _pick(seq, 256)

    inv_freq = ROPE_THETA ** (-jnp.arange(0, QK_DIM, 2, dtype=F32) / QK_DIM)
    ang = positions.astype(F32).reshape(m, 1) * inv_freq[None, :]
    cos, sin = jnp.cos(ang), jnp.sin(ang)
    cos4 = jnp.concatenate([cos, cos, cos, cos], axis=1)
    sin4 = jnp.concatenate([-sin, -sin, sin, sin], axis=1)
    lanes2 = jnp.arange(2 * LANES)
    grp = 2 * (lanes2 // LANES) + (lanes2 // ROPE_HALF) % 2
    ones_blk = (grp[:, None] == grp[None, :]).astype(BF16)

    x2 = x.reshape(m, D_MODEL)
    w_main, w_gate = _reorder_w_in(w_in)
    wa, wb, wo = w_out_a.astype(BF16), w_out_b.astype(BF16), w_o.astype(BF16)
    for l in range(depth):
        lam_init = 0.8 - 0.6 * math.exp(-0.3 * l)
        p, gates = _inproj(x2, norm_g[l][None, :], w_main, w_gate, layer=l, tm=tm, tn=2560)

        gq = _rope_lanes(q_norm_g[l]) * (QK_DIM ** -0.5 * LOG2_E)
        gk = _rope_lanes(k_norm_g[l])
        gains = jnp.concatenate([gq, gk])[None, :]
        q, k, vt = _attnprep(p, cos4, sin4, gains, ones_blk, tm=_pick(m, 4096), blk=blk)
        oa = _attention(q, k, vt, p, lambda_qk[l], attn_norm_g[l][None, :],
                        bsz=bsz, seq=seq, blk=blk, lam_init=lam_init)

        gate_bias = jnp.concatenate([igate_b[l], fgate_b[l], jnp.zeros((LANES - 2 * HEADS,), F32)])[None, :]
        cw = conv_w[l].reshape(CONV_WIDTH * 8, LANES)
        cb = conv_b[l].reshape(8, LANES)
        x2 = _mlstm_out(p, gates, gate_bias, cw, cb, mlstm_norm_g[l][None, :], oa, x2, wa, wb, wo,
                        layer=l, bsz=bsz, seq=seq, chunk=chunk)
    return x2.reshape(bsz, seq, D_MODEL)
```
